```python
import math
import jax, jax.numpy as jnp
from jax import lax
import numpy as np

D_MODEL = 1024
BATCH = 16
SEQ = 4096
DEPTH = 2
DEC_BATCH = 2
DEC_SEQ = 16384
PAST_LEN = 128

GRID_W = 64
WIN_R = 8
WIN_C = 16
QCB = 16
KCB = QCB + WIN_C
D_MIX = D_MODEL
N_MEM = 256
CROSS_HEADS = 4
CROSS_DH = 64
D_CROSS = CROSS_HEADS * CROSS_DH
D_PRIM = D_MIX - D_CROSS
NA_HEADS = 12
NA_DH = D_PRIM // NA_HEADS
HG_HEADS = 6
HG_DK = 128
HG_DV = D_PRIM // HG_HEADS
HG_F = HG_HEADS * HG_DK
HG_CHUNK = 32
D_FF = 4 * D_MODEL
N_MIXERS = 2
N_A = (DEPTH + 1) // 2
N_B = DEPTH // 2
A_IN = 3 * D_PRIM + D_CROSS
B_IN = 3 * HG_F + 2 * D_PRIM + D_CROSS
ALPHA = (2 * DEPTH) ** 0.25
BETA = (8 * DEPTH) ** -0.25
LN_EPS = 1e-5
RMS_EPS = 1e-6

kernel_name = "hybrid_natten_hgrn2_memory_encoder"


def layer_norm(x, g, b):
    xf = x.astype(jnp.float32)
    mu = jnp.mean(xf, axis=-1, keepdims=True)
    var = jnp.mean(jnp.square(xf - mu), axis=-1, keepdims=True)
    return ((xf - mu) * lax.rsqrt(var + LN_EPS) * g + b).astype(x.dtype)


def neighborhood_attention(q, k, v, rpb):
    B, T, H, Dh = q.shape
    rows = T // GRID_W
    kr = min(WIN_R, rows)
    n_cb = GRID_W // QCB
    q = q.reshape(B, rows, n_cb, QCB, H, Dh) * (Dh ** -0.5)
    k = k.reshape(B, rows, GRID_W, H, Dh)
    v = v.reshape(B, rows, GRID_W, H, Dh)
    qcol = jnp.arange(GRID_W).reshape(n_cb, QCB)
    cstart = jnp.clip(qcol - WIN_C // 2, 0, GRID_W - WIN_C)
    kcol = jnp.clip(jnp.arange(n_cb) * QCB - WIN_C // 2, 0, GRID_W - KCB)[:, None] + jnp.arange(KCB)
    kc = kcol[:, None, :]
    col_in = (kc >= cstart[..., None]) & (kc < cstart[..., None] + WIN_C)
    dcol = jnp.clip(kc - qcol[..., None] + WIN_C - 1, 0, 2 * WIN_C - 2)
    rpb_c = rpb[:, :, dcol]

    def one_row(r):
        rs = jnp.clip(r - kr // 2, 0, rows - kr)
        q_r = lax.dynamic_index_in_dim(q, r, axis=1, keepdims=False)
        k_r = lax.dynamic_slice_in_dim(k, rs, kr, axis=1)[:, :, kcol]
        v_r = lax.dynamic_slice_in_dim(v, rs, kr, axis=1)[:, :, kcol]
        drow = rs + jnp.arange(kr) - r + WIN_R - 1
        bias = jnp.transpose(jnp.take(rpb_c, drow, axis=1), (0, 2, 3, 1, 4))
        s = jnp.einsum('bnqhd,bknchd->bhnqkc', q_r, k_r).astype(jnp.float32) + bias
        s = jnp.where(col_in[:, :, None, :], s, -jnp.inf)
        p = jax.nn.softmax(s.reshape(s.shape[:4] + (kr * KCB,)), axis=-1).reshape(s.shape).astype(v.dtype)
        return jnp.einsum('bhnqkc,bknchd->bnqhd', p, v_r)

    out = lax.map(one_row, jnp.arange(rows))
    return jnp.moveaxis(out, 0, 1).reshape(B, T, H, Dh)


def hgrn2_chunk_scan(q, k, v, log_f):
    B, T, H, DK = q.shape
    DV = v.shape[-1]
    n = T // HG_CHUNK

    def to_chunks(a):
        return jnp.transpose(a.reshape(B, n, HG_CHUNK, H, a.shape[-1]), (1, 0, 3, 2, 4))

    lower = jnp.tril(jnp.ones((HG_CHUNK, HG_CHUNK), dtype=bool))

    def step(S, xs):
        qc, kc, vc, gc = xs
        b = jnp.cumsum(gc, axis=2)
        b_last = b[:, :, -1:]
        o_inter = jnp.einsum('bhtk,bhkv->bhtv', qc * jnp.exp(b), S)
        diff = jnp.where(lower[:, :, None], b[:, :, :, None, :] - b[:, :, None, :, :], -jnp.inf)
        A = jnp.einsum('bhtk,bhsk,bhtsk->bhts', qc, kc, jnp.exp(diff))
        o_intra = jnp.einsum('bhts,bhsv->bhtv', A, vc)
        S = jnp.exp(b_last)[..., 0, :, None] * S + jnp.einsum('bhsk,bhsv->bhkv', kc * jnp.exp(b_last - b), vc)
        return S, o_inter + o_intra

    S0 = jnp.zeros((B, H, DK, DV), jnp.float32)
    _, outs = lax.scan(step, S0, (to_chunks(q), to_chunks(k), to_chunks(v), to_chunks(log_f)))
    return jnp.transpose(outs, (1, 0, 3, 2, 4)).reshape(B, T, H, DV)


def hgrn2_bidirectional(q, i, f_fwd, f_bwd, gate, lb, norm_w):
    B, T, _ = q.shape
    f32 = jnp.float32
    qh = jax.nn.silu(q.astype(f32)).reshape(B, T, HG_HEADS, HG_DK)
    vh = i.astype(f32).reshape(B, T, HG_HEADS, HG_DV)
    lbh = lb.reshape(HG_HEADS, HG_DK)

    def key_and_logf(f_logit):
        f = lbh + (1.0 - lbh) * jax.nn.sigmoid(f_logit.astype(f32).reshape(B, T, HG_HEADS, HG_DK))
        return 1.0 - f, jnp.log(f)

    k_f, g_f = key_and_logf(f_fwd)
    k_b, g_b = key_and_logf(f_bwd)
    o_fwd = hgrn2_chunk_scan(qh, k_f, vh, g_f)
    rev = lambda a: jnp.flip(a, axis=1)
    o_bwd = rev(hgrn2_chunk_scan(rev(qh), rev(k_b), rev(vh), rev(g_b)))
    o = o_fwd + o_bwd
    o = o * lax.rsqrt(jnp.mean(o * o, axis=-1, keepdims=True) + RMS_EPS) * norm_w
    return o.reshape(B, T, D_PRIM) * jax.nn.silu(gate.astype(f32))


def memory_attention(q, mem_k, mem_v):
    B, T, _ = q.shape
    qh = q.reshape(B, T, CROSS_HEADS, CROSS_DH) * (CROSS_DH ** -0.5)
    kh = mem_k.reshape(B, N_MEM, CROSS_HEADS, CROSS_DH)
    vh = mem_v.reshape(B, N_MEM, CROSS_HEADS, CROSS_DH)
    s = jnp.einsum('bthd,bmhd->bhtm', qh, kh).astype(jnp.float32)
    p = jax.nn.softmax(s, axis=-1).astype(vh.dtype)
    return jnp.einsum('bhtm,bmhd->bthd', p, vh).reshape(B, T, D_CROSS)


def trunk(x, mem, lower_bounds, w_mem_kv, w_in_a, rpb, w_in_b, hg_norm_w,
          w_out, ln1_g, ln1_b, w_ff1, w_ff2, ln2_g, ln2_b):
    B, T, _ = x.shape
    mem_kv = mem @ w_mem_kv
    mem_k, mem_v = mem_kv[..., :D_CROSS], mem_kv[..., D_CROSS:]
    for layer in range(DEPTH):
        j = layer // N_MIXERS
        if layer % N_MIXERS == 0:
            h = x @ w_in_a[j]
            qa, ka, va, qm = jnp.split(h, [D_PRIM, 2 * D_PRIM, 3 * D_PRIM], axis=-1)
            heads = lambda a: a.reshape(B, T, NA_HEADS, NA_DH)
            prim = neighborhood_attention(heads(qa), heads(ka), heads(va), rpb[j]).reshape(B, T, D_PRIM)
        else:
            h = x @ w_in_b[j]
            qb, ffw, fbw, ib, gb, qm = jnp.split(
                h, [HG_F, 2 * HG_F, 3 * HG_F, 3 * HG_F + D_PRIM, 3 * HG_F + 2 * D_PRIM], axis=-1)
            prim = hgrn2_bidirectional(qb, ib, ffw, fbw, gb, lower_bounds[layer], hg_norm_w[j])
        mix = jnp.concatenate([prim.astype(x.dtype), memory_attention(qm, mem_k, mem_v)], axis=-1) @ w_out[layer]
        x = layer_norm(ALPHA * x + mix, ln1_g[layer], ln1_b[layer])
        hid = jnp.square(jax.nn.relu(x @ w_ff1[layer]))
        x = layer_norm(ALPHA * x + hid @ w_ff2[layer], ln2_g[layer], ln2_b[layer])
    return x


def setup_inputs(seed: int = 0) -> dict:
    key = jax.random.key(seed)
    ks = jax.random.split(key, 20)
    nrm = lambda k, shape, scale: jax.random.normal(k, shape, jnp.float32) * scale
    return {
        "x_prompt": nrm(ks[0], (BATCH, SEQ, D_MODEL), 1.0),
        "x_sample": nrm(ks[1], (DEC_BATCH, DEC_SEQ, D_MODEL), 1.0),
        "mem_prompt": nrm(ks[2], (BATCH, N_MEM, D_MODEL), 1.0),
        "mem_sample": nrm(ks[3], (DEC_BATCH, N_MEM, D_MODEL), 1.0),
        "w_mem_kv": nrm(ks[4], (D_MODEL, 2 * D_CROSS), D_MODEL ** -0.5),
        "w_in_a": nrm(ks[5], (N_A, D_MODEL, A_IN), D_MODEL ** -0.5),
        "rpb": nrm(ks[6], (N_A, NA_HEADS, 2 * WIN_R - 1, 2 * WIN_C - 1), 0.02),
        "w_in_b": nrm(ks[7], (N_B, D_MODEL, B_IN), D_MODEL ** -0.5),
        "lb_logits": nrm(ks[8], (DEPTH, HG_F), 0.1),
        "hg_norm_w": 1.0 + nrm(ks[9], (N_B, HG_DV), 0.02),
        "w_out": nrm(ks[10], (DEPTH, D_MIX, D_MODEL), BETA * D_MIX ** -0.5),
        "ln1_g": 1.0 + nrm(ks[11], (DEPTH, D_MODEL), 0.02),
        "ln1_b": nrm(ks[12], (DEPTH, D_MODEL), 0.02),
        "w_ff1": nrm(ks[13], (DEPTH, D_MODEL, D_FF), D_MODEL ** -0.5),
        "w_ff2": nrm(ks[14], (DEPTH, D_FF, D_MODEL), BETA * D_FF ** -0.5),
        "ln2_g": 1.0 + nrm(ks[15], (DEPTH, D_MODEL), 0.02),
        "ln2_b": nrm(ks[16], (DEPTH, D_MODEL), 0.02),
    }


def reference(x_prompt, x_sample, mem_prompt, mem_sample, w_mem_kv, w_in_a, rpb, w_in_b,
              lb_logits, hg_norm_w, w_out, ln1_g, ln1_b, w_ff1, w_ff2, ln2_g, ln2_b):
    p = jax.nn.softmax(lb_logits.astype(jnp.float32), axis=0)
    lower_bounds = jnp.cumsum(p, axis=0) - p[0]
    y_prompt = trunk(x_prompt, mem_prompt, lower_bounds, w_mem_kv, w_in_a, rpb, w_in_b, hg_norm_w,
                     w_out, ln1_g, ln1_b, w_ff1, w_ff2, ln2_g, ln2_b)
    y_sample = trunk(x_sample, mem_sample, lower_bounds, w_mem_kv, w_in_a, rpb, w_in_b, hg_norm_w,
                     w_out, ln1_g, ln1_b, w_ff1, w_ff2, ln2_g, ln2_b)
    return (y_prompt, y_sample)
```

```python
import functools

import jax
import jax.numpy as jnp
from jax import lax
from jax.experimental import pallas as pl
from jax.experimental.pallas import tpu as pltpu

F32 = jnp.float32
BF16 = jnp.bfloat16

D_MODEL = 1024
DEPTH = 2
GRID_W = 64
WIN_R = 8
WIN_C = 16
N_MEM = 256
CROSS_HEADS = 4
CROSS_DH = 64
D_CROSS = CROSS_HEADS * CROSS_DH
D_PRIM = D_MODEL - D_CROSS
NA_HEADS = 12
NA_DH = D_PRIM // NA_HEADS
HG_HEADS = 6
HG_DK = 128
HG_DV = D_PRIM // HG_HEADS
HG_F = HG_HEADS * HG_DK
D_FF = 4 * D_MODEL
ALPHA = (2 * DEPTH) ** 0.25
LN_EPS = 1e-5
RMS_EPS = 1e-6

LANES = 128
SUBLANES = 8
VMEM_LIMIT = 56 * 1024 * 1024

ROW_TILE = 512
NA_ROWS = 8
NA_TOK = NA_ROWS * GRID_W
NA_PAIRS = NA_HEADS // 2
HG_CHUNK = 32
HG_SUB = SUBLANES
HG_NSUB = HG_CHUNK // HG_SUB
HG_BLOCK = 256
FF_CHUNK = 1024
ATT_SUB = 128


def _params(*sem):
    return pltpu.CompilerParams(dimension_semantics=sem, vmem_limit_bytes=VMEM_LIMIT)


def _resident(shape):
    zeros = (0,) * len(shape)
    return pl.BlockSpec(shape, lambda *_: zeros, pipeline_mode=pl.Buffered(1))


def _proj_kernel(x_ref, w_ref, o_ref, *, n_chunk):
    xb = x_ref[...].astype(BF16)
    for c in range(0, o_ref.shape[1], n_chunk):
        acc = jnp.dot(xb, w_ref[:, c:c + n_chunk], preferred_element_type=F32)
        o_ref[:, c:c + n_chunk] = acc.astype(o_ref.dtype)


def _proj(x2d, w, tm):
    n_rows, d = x2d.shape
    n_out = w.shape[1]
    tm = min(tm, n_rows)
    assert n_rows % tm == 0
    n_chunk = 512 if n_out % 512 == 0 else n_out
    return pl.pallas_call(
        functools.partial(_proj_kernel, n_chunk=n_chunk),
        grid=(n_rows // tm,),
        in_specs=[pl.BlockSpec((tm, d), lambda i: (i, 0)), _resident((d, n_out))],
        out_specs=pl.BlockSpec((tm, n_out), lambda i: (i, 0)),
        out_shape=jax.ShapeDtypeStruct((n_rows, n_out), BF16),
        compiler_params=_params("parallel"),
    )(x2d, w)


def _na_bias_table(rpb):
    qc = jnp.arange(GRID_W)[:, None]
    kc = jnp.arange(GRID_W)[None, :]
    cstart = jnp.clip(qc - WIN_C // 2, 0, GRID_W - WIN_C)
    valid = (kc >= cstart) & (kc < cstart + WIN_C)
    dcol = jnp.clip(kc - qc + WIN_C - 1, 0, 2 * WIN_C - 2)
    t = jnp.where(valid, rpb[:, :, dcol], -jnp.inf)
    nd = 2 * WIN_R - 2
    x = jnp.stack([t[:, :nd], t[:, 1:]], axis=3)
    x = x.reshape(NA_PAIRS, 2, nd, GRID_W, 2 * GRID_W)
    return jnp.transpose(x, (0, 2, 1, 3, 4)).reshape(NA_PAIRS, nd, 2 * GRID_W, 2 * GRID_W).astype(F32)


def _na_kernel(q_ref, kp_ref, kc_ref, kn_ref, vp_ref, vc_ref, vn_ref, bias_ref, o_ref, kcat, vcat, *, rows):
    j = pl.program_id(1)
    kcat[0:NA_TOK] = kp_ref[0]
    kcat[NA_TOK:2 * NA_TOK] = kc_ref[0]
    kcat[2 * NA_TOK:3 * NA_TOK] = kn_ref[0]
    vcat[0:NA_TOK] = vp_ref[0]
    vcat[NA_TOK:2 * NA_TOK] = vc_ref[0]
    vcat[2 * NA_TOK:3 * NA_TOK] = vn_ref[0]

    lo = lax.broadcasted_iota(jnp.int32, (1, LANES), 1) < NA_DH
    n_keys = WIN_R * GRID_W

    def row_body(i, carry):
        r = j * NA_ROWS + i
        rs = jnp.clip(r - WIN_R // 2, 0, rows - WIN_R)
        delta = r - rs
        start = pl.multiple_of((rs - j * NA_ROWS + NA_ROWS) * GRID_W, GRID_W)
        qrow = pl.ds(pl.multiple_of(i * GRID_W, GRID_W), GRID_W)
        for p in range(NA_PAIRS):
            cols = slice(p * LANES, (p + 1) * LANES)
            q = q_ref[0, qrow, cols] * (NA_DH ** -0.5)
            zero = jnp.zeros_like(q)
            q2 = jnp.concatenate([jnp.where(lo, q, zero), jnp.where(lo, zero, q)], axis=0)
            kk = kcat[pl.ds(start, n_keys), cols]
            vv = vcat[pl.ds(start, n_keys), cols]
            s = lax.dot_general(q2, kk, (((1,), (1,)), ((), ())), preferred_element_type=F32)
            bias = jnp.concatenate(
                [bias_ref[p, WIN_R - 1 - delta + 2 * m] for m in range(WIN_R // 2)], axis=1)
            s = s + bias
            mx = jnp.max(s, axis=-1, keepdims=True)
            e = jnp.exp(s - mx)
            den = jnp.sum(e, axis=-1, keepdims=True)
            o = jnp.dot(e.astype(BF16), vv, preferred_element_type=F32) / den
            out = jnp.where(lo, o[0:GRID_W], o[GRID_W:2 * GRID_W])
            o_ref[0, qrow, cols] = out.astype(o_ref.dtype)
        return carry

    lax.fori_loop(0, NA_ROWS, row_body, 0)


def _na_attn(h, bias, rows):
    bsz, t, _ = h.shape
    assert rows % NA_ROWS == 0 and rows >= WIN_R and t == rows * GRID_W
    nb = rows // NA_ROWS
    blk = (1, NA_TOK, D_PRIM)

    def spec(col, shift):
        return pl.BlockSpec(blk, lambda b, j: (b, jnp.clip(j + shift, 0, nb - 1), col))

    return pl.pallas_call(
        functools.partial(_na_kernel, rows=rows),
        grid=(bsz, nb),
        in_specs=[spec(0, 0), spec(1, -1), spec(1, 0), spec(1, 1), spec(2, -1), spec(2, 0), spec(2, 1),
                  _resident(bias.shape)],
        out_specs=pl.BlockSpec(blk, lambda b, j: (b, j, 0)),
        out_shape=jax.ShapeDtypeStruct((bsz, t, D_PRIM), BF16),
        scratch_shapes=[pltpu.VMEM((3 * NA_TOK, D_PRIM), BF16), pltpu.VMEM((3 * NA_TOK, D_PRIM), BF16)],
        compiler_params=_params("parallel", "parallel"),
    )(h, h, h, h, h, h, h, bias)


def _hgrn_consts():
    t = lax.broadcasted_iota(jnp.int32, (HG_CHUNK, HG_CHUNK), 0)
    s = lax.broadcasted_iota(jnp.int32, (HG_CHUNK, HG_CHUNK), 1)
    tri = jnp.where(s <= t, 1.0, 0.0).astype(BF16)
    flip = jnp.where(s == HG_CHUNK - 1 - t, 1.0, 0.0).astype(BF16)
    kr = lax.broadcasted_iota(jnp.int32, (2 * LANES, LANES), 0)
    kl = lax.broadcasted_iota(jnp.int32, (2 * LANES, LANES), 1)
    fold = jnp.where(kr // LANES == kl // HG_CHUNK, 1.0, 0.0).astype(BF16)
    trow = lax.broadcasted_iota(jnp.int32, (HG_CHUNK, LANES), 0)
    lane = lax.broadcasted_iota(jnp.int32, (HG_CHUNK, LANES), 1)
    key = lane & (HG_CHUNK - 1)
    same = (key // HG_SUB == trow // HG_SUB) & (lane < 2 * HG_CHUNK)
    diag_hit = [same & (key == trow - jnp.where(lane < HG_CHUNK, d, d + 1)) for d in range(0, HG_SUB, 2)]
    sub_lane = lax.broadcasted_iota(jnp.int32, (HG_SUB, LANES), 1)
    off_cols = [(sub_lane >= j * HG_SUB) & (sub_lane < (j + 1) * HG_SUB) for j in range(HG_NSUB)]
    return tri, flip, fold, diag_hit, off_cols


def _hgrn_chunk(ql, fl, v, lb, st_ref, consts):
    tri, _, fold, diag_hit, off_cols = consts
    qs = ql * jax.nn.sigmoid(ql)
    f = lb + (1.0 - lb) * jax.nn.sigmoid(fl)
    k = 1.0 - f
    g = jnp.log(f)
    g_hi = g.astype(BF16)
    rem = g - g_hi.astype(F32)
    g_mid = rem.astype(BF16)
    g_lo = (rem - g_mid.astype(F32)).astype(BF16)
    b = (jnp.dot(tri, g_hi, preferred_element_type=F32) + jnp.dot(tri, g_mid, preferred_element_type=F32)
         + jnp.dot(tri, g_lo, preferred_element_type=F32))

    def blk(a, i):
        return a[i * HG_SUB:(i + 1) * HG_SUB]

    def row(a, r):
        return a[r:r + 1]

    ends = [row(b, (i + 1) * HG_SUB - 1) for i in range(HG_NSUB)]
    b_last = ends[-1]
    q_hat = [blk(qs, 0) * jnp.exp(blk(b, 0))]
    q_hat += [blk(qs, i) * jnp.exp(blk(b, i) - ends[i - 1]) for i in range(1, HG_NSUB)]
    k_hat = [blk(k, i) * jnp.exp(ends[i] - blk(b, i)) for i in range(HG_NSUB)]
    q_bar = [q_hat[0]] + [q_hat[i] * jnp.exp(ends[i - 1]) for i in range(1, HG_NSUB)]
    k_bar = [k_hat[i] * jnp.exp(b_last - ends[i]) for i in range(HG_NSUB - 1)] + [k_hat[-1]]
    pairs = [(i, j) for i in range(1, HG_NSUB) for j in range(i)]
    lhs_off = jnp.concatenate(
        [q_hat[i] if i == j + 1 else q_hat[i] * jnp.exp(ends[i - 1] - ends[j]) for i, j in pairs],
        axis=0).astype(BF16)
    q_bar = jnp.concatenate(q_bar, axis=0).astype(BF16)
    k_hat = jnp.concatenate(k_hat, axis=0).astype(BF16)
    k_bar = jnp.concatenate(k_bar, axis=0).astype(BF16)
    vb = v.astype(BF16)
    decay = jnp.exp(b_last)

    terms = [qs * k]
    w = None
    for d in range(1, HG_SUB):
        fr = f if d == 1 else pltpu.roll(f, d - 1, 0)
        w = fr if w is None else w * fr
        terms.append(qs * pltpu.roll(k, d, 0) * w)
    terms = [x.astype(BF16) for x in terms]

    zpad = jnp.zeros((LANES - HG_CHUNK, LANES), BF16)
    outs = []
    for h in range(HG_HEADS):
        cs = slice(h * LANES, (h + 1) * LANES)
        r_off = lax.dot_general(lhs_off[:, cs], jnp.concatenate([k_hat[:, cs], zpad], axis=0),
                                (((1,), (1,)), ((), ())), preferred_element_type=F32)
        a_rows = [jnp.zeros((HG_SUB, LANES), F32)]
        for i in range(1, HG_NSUB):
            acc = jnp.zeros((HG_SUB, LANES), F32)
            for j in range(i):
                pi = pairs.index((i, j))
                acc = jnp.where(off_cols[j], r_off[pi * HG_SUB:(pi + 1) * HG_SUB], acc)
            a_rows.append(acc)
        a = jnp.concatenate(a_rows, axis=0)
        z = jnp.concatenate(
            [jnp.concatenate([terms[d][:, cs], terms[d + 1][:, cs]], axis=1) for d in range(0, HG_SUB, 2)],
            axis=0)
        rd = jnp.dot(z, fold, preferred_element_type=F32)
        for n in range(HG_SUB // 2):
            a = a + jnp.where(diag_hit[n], rd[n * HG_CHUNK:(n + 1) * HG_CHUNK], 0.0)
        v_h = vb[:, cs]
        v2 = jnp.concatenate([v_h, v_h, jnp.zeros((LANES - 2 * HG_CHUNK, LANES), BF16)], axis=0)
        st = st_ref[h]
        o = (lax.dot_general(q_bar[:, cs], st.astype(BF16), (((1,), (1,)), ((), ())),
                             preferred_element_type=F32)
             + jnp.dot(a.astype(BF16), v2, preferred_element_type=F32))
        upd = lax.dot_general(v_h, k_bar[:, cs], (((0,), (0,)), ((), ())), preferred_element_type=F32)
        st_ref[h] = st * decay[:, cs] + upd
        outs.append(o)
    return jnp.concatenate(outs, axis=1)


def _hgrn_kernel(lbl_ref, qf_ref, ff_ref, vf_ref, qb_ref, fb_ref, vb_ref, of_ref, ob_ref, st_f, st_b, *, layer):
    @pl.when(pl.program_id(1) == 0)
    def _():
        st_f[...] = jnp.zeros_like(st_f)
        st_b[...] = jnp.zeros_like(st_b)

    logits = lbl_ref[...]
    mx = jnp.max(logits, axis=0, keepdims=True)
    ex = jnp.exp(logits - mx)
    prob = ex / jnp.sum(ex, axis=0, keepdims=True)
    lb = jnp.sum(prob[0:layer + 1], axis=0, keepdims=True) - prob[0:1]

    consts = _hgrn_consts()
    flip = consts[1]
    n_chunks = HG_BLOCK // HG_CHUNK

    def body(c, carry):
        rf = pl.ds(pl.multiple_of(c * HG_CHUNK, HG_CHUNK), HG_CHUNK)
        o = _hgrn_chunk(qf_ref[0, rf, :].astype(F32), ff_ref[0, rf, :].astype(F32),
                        vf_ref[0, rf, :].astype(F32), lb, st_f, consts)
        of_ref[0, rf, :] = o.astype(of_ref.dtype)
        rb = pl.ds(pl.multiple_of((n_chunks - 1 - c) * HG_CHUNK, HG_CHUNK), HG_CHUNK)
        rev = lambda ref: jnp.dot(flip, ref[0, rb, :], preferred_element_type=F32)
        o = _hgrn_chunk(rev(qb_ref), rev(fb_ref), rev(vb_ref), lb, st_b, consts)
        ob_ref[0, rb, :] = jnp.dot(flip, o.astype(BF16), preferred_element_type=F32).astype(ob_ref.dtype)
        return carry

    lax.fori_loop(0, n_chunks, body, 0)


def _hgrn_scan(h, lb_logits, layer):
    bsz, t, _ = h.shape
    assert t % HG_BLOCK == 0
    nb = t // HG_BLOCK
    blk = (1, HG_BLOCK, HG_F)
    fwd = lambda col: pl.BlockSpec(blk, lambda b, j: (b, j, col))
    bwd = lambda col: pl.BlockSpec(blk, lambda b, j: (b, nb - 1 - j, col))
    out_sd = jax.ShapeDtypeStruct((bsz, t, D_PRIM), BF16)
    return pl.pallas_call(
        functools.partial(_hgrn_kernel, layer=layer),
        grid=(bsz, nb),
        in_specs=[_resident(lb_logits.shape), fwd(0), fwd(1), fwd(3), bwd(0), bwd(2), bwd(3)],
        out_specs=[fwd(0), bwd(0)],
        out_shape=[out_sd, out_sd],
        scratch_shapes=[pltpu.VMEM((HG_HEADS, HG_DV, HG_DK), F32), pltpu.VMEM((HG_HEADS, HG_DV, HG_DK), F32)],
        compiler_params=_params("parallel", "arbitrary"),
    )(lb_logits, h, h, h, h, h, h)


def _layer_norm(y, g, b):
    mu = jnp.mean(y, axis=-1, keepdims=True)
    yc = y - mu
    var = jnp.mean(yc * yc, axis=-1, keepdims=True)
    return yc * lax.rsqrt(var + LN_EPS) * g + b


def _mem_attention(qm_ref, mkv_ref, mem_out):
    lo = lax.broadcasted_iota(jnp.int32, (1, LANES), 1) < CROSS_DH

    def sub_body(t, carry):
        rws = pl.ds(pl.multiple_of(t * ATT_SUB, ATT_SUB), ATT_SUB)
        for p in range(CROSS_HEADS // 2):
            cols = slice(p * LANES, (p + 1) * LANES)
            q = qm_ref[0, rws, cols] * (CROSS_DH ** -0.5)
            zero = jnp.zeros_like(q)
            q2 = jnp.concatenate([jnp.where(lo, q, zero), jnp.where(lo, zero, q)], axis=0)
            mk = mkv_ref[0, :, cols]
            mv = mkv_ref[0, :, D_CROSS + p * LANES:D_CROSS + (p + 1) * LANES]
            s = lax.dot_general(q2, mk, (((1,), (1,)), ((), ())), preferred_element_type=F32)
            mx = jnp.max(s, axis=-1, keepdims=True)
            e = jnp.exp(s - mx)
            den = jnp.sum(e, axis=-1, keepdims=True)
            o = jnp.dot(e.astype(BF16), mv, preferred_element_type=F32) / den
            mem_out[rws, cols] = jnp.where(lo, o[0:ATT_SUB], o[ATT_SUB:2 * ATT_SUB]).astype(BF16)
        return carry

    lax.fori_loop(0, qm_ref.shape[1] // ATT_SUB, sub_body, 0)


def _mix_ffn_tail(x_ref, prim, mem_out, wo_ref, g1_ref, b1_ref, w1_ref, w2_ref, g2_ref, b2_ref, o_ref):
    mix = (jnp.dot(prim, wo_ref[0:D_PRIM, :], preferred_element_type=F32)
           + jnp.dot(mem_out[...], wo_ref[D_PRIM:D_MODEL, :], preferred_element_type=F32))
    x1 = _layer_norm(ALPHA * x_ref[0] + mix, g1_ref[...], b1_ref[...])
    x1b = x1.astype(BF16)
    acc = jnp.zeros(x1.shape, F32)
    for c in range(0, D_FF, FF_CHUNK):
        hid = jnp.dot(x1b, w1_ref[:, c:c + FF_CHUNK], preferred_element_type=F32)
        hid = jnp.square(jnp.maximum(hid, 0.0)).astype(BF16)
        acc = acc + jnp.dot(hid, w2_ref[c:c + FF_CHUNK, :], preferred_element_type=F32)
    o_ref[0] = _layer_norm(ALPHA * x1 + acc, g2_ref[...], b2_ref[...])


def _mix_ffn_a_kernel(x_ref, prim_ref, qm_ref, mkv_ref, wo_ref, g1_ref, b1_ref, w1_ref, w2_ref, g2_ref, b2_ref,
                      o_ref, mem_out):
    _mem_attention(qm_ref, mkv_ref, mem_out)
    _mix_ffn_tail(x_ref, prim_ref[0], mem_out, wo_ref, g1_ref, b1_ref, w1_ref, w2_ref, g2_ref, b2_ref, o_ref)


def _mix_ffn_b_kernel(x_ref, of_ref, ob_ref, gate_ref, nw_ref, qm_ref, mkv_ref, wo_ref, g1_ref, b1_ref, w1_ref,
                      w2_ref, g2_ref, b2_ref, o_ref, mem_out):
    _mem_attention(qm_ref, mkv_ref, mem_out)
    o = of_ref[0].astype(F32) + ob_ref[0].astype(F32)
    gate = gate_ref[0].astype(F32)
    gate = gate * jax.nn.sigmoid(gate)
    nw = nw_ref[...]
    heads = []
    for h in range(HG_HEADS):
        cs = slice(h * HG_DV, (h + 1) * HG_DV)
        oh = o[:, cs]
        oh = oh * lax.rsqrt(jnp.mean(oh * oh, axis=-1, keepdims=True) + RMS_EPS) * nw
        heads.append((oh * gate[:, cs]).astype(BF16))
    prim = jnp.concatenate(heads, axis=1)
    _mix_ffn_tail(x_ref, prim, mem_out, wo_ref, g1_ref, b1_ref, w1_ref, w2_ref, g2_ref, b2_ref, o_ref)


def _mix_ffn(x, mixer_in, h, qm_col, mem_kv, w_out, ln1_g, ln1_b, w_ff1, w_ff2, ln2_g, ln2_b, norm_w=None):
    bsz, t, d = x.shape
    tm = ROW_TILE
    assert t % tm == 0
    tile = lambda w, col: pl.BlockSpec((1, tm, w), lambda b, i: (b, i, col))
    vec = lambda a: a.reshape(1, -1)
    common_specs = [tile(D_CROSS, qm_col), pl.BlockSpec((1, N_MEM, 2 * D_CROSS), lambda b, i: (b, 0, 0)),
                    _resident(w_out.shape), _resident((1, d)), _resident((1, d)), _resident(w_ff1.shape),
                    _resident(w_ff2.shape), _resident((1, d)), _resident((1, d))]
    common_args = [h, mem_kv, w_out, vec(ln1_g), vec(ln1_b), w_ff1, w_ff2, vec(ln2_g), vec(ln2_b)]
    if norm_w is None:
        body = _mix_ffn_a_kernel
        specs = [tile(d, 0), tile(D_PRIM, 0)] + common_specs
        args = [x, mixer_in[0]] + common_args
    else:
        body = _mix_ffn_b_kernel
        specs = [tile(d, 0), tile(D_PRIM, 0), tile(D_PRIM, 0), tile(D_PRIM, 4), _resident((1, HG_DV))] + common_specs
        args = [x, mixer_in[0], mixer_in[1], h, vec(norm_w)] + common_args
    return pl.pallas_call(
        body,
        grid=(bsz, t // tm),
        in_specs=specs,
        out_specs=tile(d, 0),
        out_shape=jax.ShapeDtypeStruct((bsz, t, d), F32),
        scratch_shapes=[pltpu.VMEM((tm, D_CROSS), BF16)],
        compiler_params=_params("parallel", "parallel"),
    )(*args)


def _trunk(x, mem, w_mem_kv, w_in_a, na_bias, w_in_b, lb_logits, hg_norm_w, w_out, ln1_g, ln1_b, w_ff1, w_ff2,
           ln2_g, ln2_b):
    bsz, t, d = x.shape
    rows = t // GRID_W
    mem_kv = _proj(mem.reshape(bsz * N_MEM, d), w_mem_kv, ROW_TILE).reshape(bsz, N_MEM, 2 * D_CROSS)
    for layer in range(DEPTH):
        j = layer // 2
        if layer % 2 == 0:
            h = _proj(x.reshape(bsz * t, d), w_in_a[j], ROW_TILE).reshape(bsz, t, -1)
            mixer_in = (_na_attn(h, na_bias[j], rows),)
            qm_col, norm_w = 3 * D_PRIM // D_CROSS, None
        else:
            h = _proj(x.reshape(bsz * t, d), w_in_b[j], ROW_TILE).reshape(bsz, t, -1)
            mixer_in = _hgrn_scan(h, lb_logits, layer)
            qm_col, norm_w = (3 * HG_F + 2 * D_PRIM) // D_CROSS, hg_norm_w[j]
        x = _mix_ffn(x, mixer_in, h, qm_col, mem_kv, w_out[layer], ln1_g[layer], ln1_b[layer], w_ff1[layer],
                     w_ff2[layer], ln2_g[layer], ln2_b[layer], norm_w)
    return x


def kernel(x_prompt, x_sample, mem_prompt, mem_sample, w_mem_kv, w_in_a, rpb, w_in_b, lb_logits, hg_norm_w, w_out,
           ln1_g, ln1_b, w_ff1, w_ff2, ln2_g, ln2_b):
    bf = lambda a: a.astype(BF16)
    na_bias = jax.vmap(_na_bias_table)(rpb)
    shared = (bf(w_mem_kv), bf(w_in_a), na_bias, bf(w_in_b), lb_logits.astype(F32), hg_norm_w, bf(w_out), ln1_g,
              ln1_b, bf(w_ff1), bf(w_ff2), ln2_g, ln2_b)
    return (_trunk(x_prompt, mem_prompt, *shared), _trunk(x_sample, mem_sample, *shared))
```

```python
import functools

import jax
import jax.numpy as jnp
from jax import lax
from jax.experimental import pallas as pl
from jax.experimental.pallas import tpu as pltpu

F32 = jnp.float32
BF16 = jnp.bfloat16

D_MODEL = 1024
DEPTH = 2
GRID_W = 64
WIN_R = 8
WIN_C = 16
N_MEM = 256
CROSS_HEADS = 4
CROSS_DH = 64
D_CROSS = CROSS_HEADS * CROSS_DH
D_PRIM = D_MODEL - D_CROSS
NA_HEADS = 12
NA_DH = D_PRIM // NA_HEADS
HG_HEADS = 6
HG_DK = 128
HG_DV = D_PRIM // HG_HEADS
HG_F = HG_HEADS * HG_DK
D_FF = 4 * D_MODEL
ALPHA = (2 * DEPTH) ** 0.25
LN_EPS = 1e-5
RMS_EPS = 1e-6

LANES = 128
SUBLANES = 8
VMEM_LIMIT = 56 * 1024 * 1024

ROW_TILE = 512
NA_ROWS = 8
NA_TOK = NA_ROWS * GRID_W
NA_PAIRS = NA_HEADS // 2
HG_CHUNK = 32
HG_SUB = SUBLANES
HG_NSUB = HG_CHUNK // HG_SUB
HG_BLOCK = 256
HG_NCHUNK = HG_BLOCK // HG_CHUNK
FF_CHUNK = 1024
ATT_SUB = 128


def _params(*sem):
    return pltpu.CompilerParams(dimension_semantics=sem, vmem_limit_bytes=VMEM_LIMIT)


def _resident(shape):
    zeros = (0,) * len(shape)
    return pl.BlockSpec(shape, lambda *_: zeros, pipeline_mode=pl.Buffered(1))


def _proj_kernel(x_ref, w_ref, o_ref, *, n_chunk):
    xb = x_ref[...].astype(BF16)
    for c in range(0, o_ref.shape[1], n_chunk):
        acc = jnp.dot(xb, w_ref[:, c:c + n_chunk], preferred_element_type=F32)
        o_ref[:, c:c + n_chunk] = acc.astype(o_ref.dtype)


def _proj(x2d, w, tm):
    n_rows, d = x2d.shape
    n_out = w.shape[1]
    tm = min(tm, n_rows)
    assert n_rows % tm == 0
    n_chunk = 512 if n_out % 512 == 0 else n_out
    return pl.pallas_call(
        functools.partial(_proj_kernel, n_chunk=n_chunk),
        grid=(n_rows // tm,),
        in_specs=[pl.BlockSpec((tm, d), lambda i: (i, 0)), _resident((d, n_out))],
        out_specs=pl.BlockSpec((tm, n_out), lambda i: (i, 0)),
        out_shape=jax.ShapeDtypeStruct((n_rows, n_out), BF16),
        compiler_params=_params("parallel"),
        name="proj",
    )(x2d, w)


def _na_bias_table(rpb):
    qc = jnp.arange(GRID_W)[:, None]
    kc = jnp.arange(GRID_W)[None, :]
    cstart = jnp.clip(qc - WIN_C // 2, 0, GRID_W - WIN_C)
    valid = (kc >= cstart) & (kc < cstart + WIN_C)
    dcol = jnp.clip(kc - qc + WIN_C - 1, 0, 2 * WIN_C - 2)
    t = jnp.where(valid, rpb[:, :, dcol], -jnp.inf)
    nd = 2 * WIN_R - 2
    x = jnp.stack([t[:, :nd], t[:, 1:]], axis=3)
    x = x.reshape(NA_PAIRS, 2, nd, GRID_W, 2 * GRID_W)
    return jnp.transpose(x, (0, 2, 1, 3, 4)).reshape(NA_PAIRS, nd, 2 * GRID_W, 2 * GRID_W).astype(F32)


def _na_kernel(q_ref, kp_ref, kc_ref, kn_ref, vp_ref, vc_ref, vn_ref, bias_ref, o_ref, kcat, vcat, *, rows):
    j = pl.program_id(1)
    kcat[0:NA_TOK] = kp_ref[0]
    kcat[NA_TOK:2 * NA_TOK] = kc_ref[0]
    kcat[2 * NA_TOK:3 * NA_TOK] = kn_ref[0]
    vcat[0:NA_TOK] = vp_ref[0]
    vcat[NA_TOK:2 * NA_TOK] = vc_ref[0]
    vcat[2 * NA_TOK:3 * NA_TOK] = vn_ref[0]

    lo = lax.broadcasted_iota(jnp.int32, (1, LANES), 1) < NA_DH
    n_keys = WIN_R * GRID_W

    def row_body(i, carry):
        r = j * NA_ROWS + i
        rs = jnp.clip(r - WIN_R // 2, 0, rows - WIN_R)
        delta = r - rs
        start = pl.multiple_of((rs - j * NA_ROWS + NA_ROWS) * GRID_W, GRID_W)
        qrow = pl.ds(pl.multiple_of(i * GRID_W, GRID_W), GRID_W)
        for p in range(NA_PAIRS):
            cols = slice(p * LANES, (p + 1) * LANES)
            q = q_ref[0, qrow, cols] * (NA_DH ** -0.5)
            zero = jnp.zeros_like(q)
            q2 = jnp.concatenate([jnp.where(lo, q, zero), jnp.where(lo, zero, q)], axis=0)
            kk = kcat[pl.ds(start, n_keys), cols]
            vv = vcat[pl.ds(start, n_keys), cols]
            s = lax.dot_general(q2, kk, (((1,), (1,)), ((), ())), preferred_element_type=F32)
            bias = jnp.concatenate(
                [bias_ref[p, WIN_R - 1 - delta + 2 * m] for m in range(WIN_R // 2)], axis=1)
            s = s + bias
            mx = jnp.max(s, axis=-1, keepdims=True)
            e = jnp.exp(s - mx)
            den = jnp.sum(e, axis=-1, keepdims=True)
            o = jnp.dot(e.astype(BF16), vv, preferred_element_type=F32) / den
            out = jnp.where(lo, o[0:GRID_W], o[GRID_W:2 * GRID_W])
            o_ref[0, qrow, cols] = out.astype(o_ref.dtype)
        return carry

    lax.fori_loop(0, NA_ROWS, row_body, 0)


def _na_attn(h, bias, rows):
    bsz, t, _ = h.shape
    assert rows % NA_ROWS == 0 and rows >= WIN_R and t == rows * GRID_W
    nb = rows // NA_ROWS
    blk = (1, NA_TOK, D_PRIM)

    def spec(col, shift):
        return pl.BlockSpec(blk, lambda b, j: (b, jnp.clip(j + shift, 0, nb - 1), col))

    return pl.pallas_call(
        functools.partial(_na_kernel, rows=rows),
        grid=(bsz, nb),
        in_specs=[spec(0, 0), spec(1, -1), spec(1, 0), spec(1, 1), spec(2, -1), spec(2, 0), spec(2, 1),
                  _resident(bias.shape)],
        out_specs=pl.BlockSpec(blk, lambda b, j: (b, j, 0)),
        out_shape=jax.ShapeDtypeStruct((bsz, t, D_PRIM), BF16),
        scratch_shapes=[pltpu.VMEM((3 * NA_TOK, D_PRIM), BF16), pltpu.VMEM((3 * NA_TOK, D_PRIM), BF16)],
        compiler_params=_params("parallel", "parallel"),
        name="na_attn",
    )(h, h, h, h, h, h, h, bias)


def _hgrn_consts(rev):
    t = lax.broadcasted_iota(jnp.int32, (HG_BLOCK, HG_BLOCK), 0)
    s = lax.broadcasted_iota(jnp.int32, (HG_BLOCK, HG_BLOCK), 1)
    seen = (s >= t) if rev else (s <= t)
    tri = jnp.where(seen & (s // HG_CHUNK == t // HG_CHUNK), 1.0, 0.0).astype(BF16)
    kr = lax.broadcasted_iota(jnp.int32, (2 * LANES, LANES), 0)
    kl = lax.broadcasted_iota(jnp.int32, (2 * LANES, LANES), 1)
    fold = jnp.where(kr // LANES == kl // HG_CHUNK, 1.0, 0.0).astype(BF16)
    trow = lax.broadcasted_iota(jnp.int32, (HG_CHUNK, LANES), 0)
    lane = lax.broadcasted_iota(jnp.int32, (HG_CHUNK, LANES), 1)
    key = lane & (HG_CHUNK - 1)
    same = (key // HG_SUB == trow // HG_SUB) & (lane < 2 * HG_CHUNK)
    diag_hit = []
    for d in range(0, HG_SUB, 2):
        dist = jnp.where(lane < HG_CHUNK, d, d + 1)
        diag_hit.append(same & (key == (trow + dist if rev else trow - dist)))
    sub_lane = lax.broadcasted_iota(jnp.int32, (HG_SUB, LANES), 1)
    off_cols = [(sub_lane >= i * HG_SUB) & (sub_lane < (i + 1) * HG_SUB) for i in range(HG_NSUB)]
    return tri, fold, diag_hit, off_cols


def _hgrn_head(ql, fl, v, lb, st, consts, rev):
    tri, fold, diag_hit, off_cols = consts
    qs = ql * jax.nn.sigmoid(ql)
    f = lb + (1.0 - lb) * jax.nn.sigmoid(fl)
    k = 1.0 - f
    g = jnp.log(f)
    g_hi = g.astype(BF16)
    rem = g - g_hi.astype(F32)
    g_mid = rem.astype(BF16)
    g_lo = (rem - g_mid.astype(F32)).astype(BF16)
    b = (jnp.dot(tri, g_hi, preferred_element_type=F32) + jnp.dot(tri, g_mid, preferred_element_type=F32)
         + jnp.dot(tri, g_lo, preferred_element_type=F32))

    def pos(p):
        return HG_NSUB - 1 - p if rev else p

    def sub(a, c, p):
        r0 = c * HG_CHUNK + pos(p) * HG_SUB
        return a[r0:r0 + HG_SUB]

    def end(c, p):
        r = c * HG_CHUNK + pos(p) * HG_SUB + (0 if rev else HG_SUB - 1)
        return b[r:r + 1]

    pairs = [(i, j) for i in range(1, HG_NSUB) for j in range(i)]
    q_bar, k_hat, k_bar, lhs_off, decay = {}, {}, {}, [], []
    for c in range(HG_NCHUNK):
        ends = [end(c, p) for p in range(HG_NSUB)]
        q_hat = []
        for p in range(HG_NSUB):
            bb = sub(b, c, p)
            q_hat.append(sub(qs, c, p) * jnp.exp(bb if p == 0 else bb - ends[p - 1]))
            kh = sub(k, c, p) * jnp.exp(ends[p] - bb)
            k_hat[c, pos(p)] = kh
            q_bar[c, pos(p)] = q_hat[p] if p == 0 else q_hat[p] * jnp.exp(ends[p - 1])
            k_bar[c, pos(p)] = kh if p == HG_NSUB - 1 else kh * jnp.exp(ends[-1] - ends[p])
        lhs_off += [q_hat[i] if i == j + 1 else q_hat[i] * jnp.exp(ends[i - 1] - ends[j]) for i, j in pairs]
        decay.append(jnp.exp(ends[-1]))

    def rows_cat(parts):
        return jnp.concatenate([parts[c, i] for c in range(HG_NCHUNK) for i in range(HG_NSUB)], axis=0).astype(BF16)

    q_bar, k_hat, k_bar = rows_cat(q_bar), rows_cat(k_hat), rows_cat(k_bar)
    lhs_off = jnp.concatenate(lhs_off, axis=0).astype(BF16)
    vb = v.astype(BF16)

    shift = HG_SUB - 1 if rev else 1

    def prev(a):
        a3 = a.reshape(HG_BLOCK // HG_SUB, HG_SUB, LANES)
        return pltpu.roll(a3, shift, 1).reshape(HG_BLOCK, LANES)

    kd = k
    terms = [qs * k]
    for d in range(1, HG_SUB):
        kd = f * prev(kd)
        terms.append(qs * kd)
    terms = [x.astype(BF16) for x in terms]
    z = jnp.concatenate([jnp.concatenate([terms[d], terms[d + 1]], axis=1) for d in range(0, HG_SUB, 2)], axis=0)
    rd = jnp.dot(z, fold, preferred_element_type=F32)

    zpad = jnp.zeros((LANES - HG_CHUNK, LANES), BF16)
    vpad = jnp.zeros((LANES - 2 * HG_CHUNK, LANES), BF16)
    o_intra, upd = [], []
    for c in range(HG_NCHUNK):
        rs = slice(c * HG_CHUNK, (c + 1) * HG_CHUNK)
        n_off = len(pairs) * HG_SUB
        r_off = lax.dot_general(lhs_off[c * n_off:(c + 1) * n_off], jnp.concatenate([k_hat[rs], zpad], axis=0),
                                (((1,), (1,)), ((), ())), preferred_element_type=F32)
        a_rows = []
        for i in range(HG_NSUB):
            p = pos(i)
            acc = jnp.zeros((HG_SUB, LANES), F32)
            for j in range(p):
                n = pairs.index((p, j))
                acc = jnp.where(off_cols[pos(j)], r_off[n * HG_SUB:(n + 1) * HG_SUB], acc)
            a_rows.append(acc)
        a = jnp.concatenate(a_rows, axis=0)
        for n in range(HG_SUB // 2):
            a = a + jnp.where(diag_hit[n], rd[n * HG_BLOCK + c * HG_CHUNK:n * HG_BLOCK + (c + 1) * HG_CHUNK], 0.0)
        v_c = vb[rs]
        v2 = jnp.concatenate([v_c, v_c, vpad], axis=0)
        o_intra.append(jnp.dot(a.astype(BF16), v2, preferred_element_type=F32))
        upd.append(lax.dot_general(v_c, k_bar[rs], (((0,), (0,)), ((), ())), preferred_element_type=F32))

    outs = [None] * HG_NCHUNK
    for step in range(HG_NCHUNK):
        c = HG_NCHUNK - 1 - step if rev else step
        rs = slice(c * HG_CHUNK, (c + 1) * HG_CHUNK)
        outs[c] = o_intra[c] + lax.dot_general(q_bar[rs], st.astype(BF16), (((1,), (1,)), ((), ())),
                                               preferred_element_type=F32)
        st = st * decay[c] + upd[c]
    return jnp.concatenate(outs, axis=0), st


def _hgrn_kernel(lbl_ref, qf_ref, ff_ref, vf_ref, qb_ref, fb_ref, vb_ref, of_ref, ob_ref, st_f, st_b, lb_ref, *,
                 layer):
    @pl.when(pl.program_id(1) == 0)
    def _():
        st_f[...] = jnp.zeros_like(st_f)
        st_b[...] = jnp.zeros_like(st_b)

    logits = lbl_ref[...]
    mx = jnp.max(logits, axis=0, keepdims=True)
    ex = jnp.exp(logits - mx)
    prob = ex / jnp.sum(ex, axis=0, keepdims=True)
    lb_ref[...] = jnp.sum(prob[0:layer + 1], axis=0, keepdims=True) - prob[0:1]

    consts_f, consts_b = _hgrn_consts(False), _hgrn_consts(True)

    def head(h, carry):
        cs = pl.ds(pl.multiple_of(h * LANES, LANES), LANES)
        for q_ref, f_ref, v_ref, o_ref, st_ref, consts, rev in (
                (qf_ref, ff_ref, vf_ref, of_ref, st_f, consts_f, False),
                (qb_ref, fb_ref, vb_ref, ob_ref, st_b, consts_b, True)):
            o, st = _hgrn_head(q_ref[0, :, cs].astype(F32), f_ref[0, :, cs].astype(F32), v_ref[0, :, cs].astype(F32),
                               lb_ref[:, cs], st_ref[h], consts, rev)
            o_ref[0, :, cs] = o.astype(o_ref.dtype)
            st_ref[h] = st
        return carry

    lax.fori_loop(0, HG_HEADS, head, 0)


def _hgrn_scan(h, lb_logits, layer):
    bsz, t, _ = h.shape
    assert t % HG_BLOCK == 0
    nb = t // HG_BLOCK
    blk = (1, HG_BLOCK, HG_F)
    fwd = lambda col: pl.BlockSpec(blk, lambda b, j: (b, j, col))
    bwd = lambda col: pl.BlockSpec(blk, lambda b, j: (b, nb - 1 - j, col))
    out_sd = jax.ShapeDtypeStruct((bsz, t, D_PRIM), BF16)
    state = pltpu.VMEM((HG_HEADS, HG_DV, HG_DK), F32)
    return pl.pallas_call(
        functools.partial(_hgrn_kernel, layer=layer),
        grid=(bsz, nb),
        in_specs=[_resident(lb_logits.shape), fwd(0), fwd(1), fwd(3), bwd(0), bwd(2), bwd(3)],
        out_specs=[fwd(0), bwd(0)],
        out_shape=[out_sd, out_sd],
        scratch_shapes=[state, state, pltpu.VMEM((1, HG_F), F32)],
        compiler_params=_params("parallel", "arbitrary"),
        name="hgrn_scan",
    )(lb_logits, h, h, h, h, h, h)


def _layer_norm(y, g, b):
    mu = jnp.mean(y, axis=-1, keepdims=True)
    yc = y - mu
    var = jnp.mean(yc * yc, axis=-1, keepdims=True)
    return yc * lax.rsqrt(var + LN_EPS) * g + b


def _mem_attention(qm_ref, mkv_ref, mem_out):
    lo = lax.broadcasted_iota(jnp.int32, (1, LANES), 1) < CROSS_DH

    for t in range(qm_ref.shape[1] // ATT_SUB):
        rws = slice(t * ATT_SUB, (t + 1) * ATT_SUB)
        for p in range(CROSS_HEADS // 2):
            cols = slice(p * LANES, (p + 1) * LANES)
            q = qm_ref[0, rws, cols] * (CROSS_DH ** -0.5)
            zero = jnp.zeros_like(q)
            q2 = jnp.concatenate([jnp.where(lo, q, zero), jnp.where(lo, zero, q)], axis=0)
            mk = mkv_ref[0, :, cols]
            mv = mkv_ref[0, :, D_CROSS + p * LANES:D_CROSS + (p + 1) * LANES]
            s = lax.dot_general(q2, mk, (((1,), (1,)), ((), ())), preferred_element_type=F32)
            mx = jnp.max(s, axis=-1, keepdims=True)
            e = jnp.exp(s - mx)
            den = jnp.sum(e, axis=-1, keepdims=True)
            o = jnp.dot(e.astype(BF16), mv, preferred_element_type=F32) / den
            mem_out[rws, cols] = jnp.where(lo, o[0:ATT_SUB], o[ATT_SUB:2 * ATT_SUB]).astype(BF16)


def _mix_ffn_tail(x_ref, prim, mem_out, wo_ref, g1_ref, b1_ref, w1_ref, w2_ref, g2_ref, b2_ref, o_ref):
    mix = (jnp.dot(prim, wo_ref[0:D_PRIM, :], preferred_element_type=F32)
           + jnp.dot(mem_out[...], wo_ref[D_PRIM:D_MODEL, :], preferred_element_type=F32))
    x1 = _layer_norm(ALPHA * x_ref[0] + mix, g1_ref[...], b1_ref[...])
    x1b = x1.astype(BF16)
    acc = jnp.zeros(x1.shape, F32)
    for c in range(0, D_FF, FF_CHUNK):
        hid = jnp.dot(x1b, w1_ref[:, c:c + FF_CHUNK], preferred_element_type=F32)
        hid = jnp.square(jnp.maximum(hid, 0.0)).astype(BF16)
        acc = acc + jnp.dot(hid, w2_ref[c:c + FF_CHUNK, :], preferred_element_type=F32)
    o_ref[0] = _layer_norm(ALPHA * x1 + acc, g2_ref[...], b2_ref[...])


def _mix_ffn_a_kernel(x_ref, prim_ref, qm_ref, mkv_ref, wo_ref, g1_ref, b1_ref, w1_ref, w2_ref, g2_ref, b2_ref,
                      o_ref, mem_out):
    _mem_attention(qm_ref, mkv_ref, mem_out)
    _mix_ffn_tail(x_ref, prim_ref[0], mem_out, wo_ref, g1_ref, b1_ref, w1_ref, w2_ref, g2_ref, b2_ref, o_ref)


def _mix_ffn_b_kernel(x_ref, of_ref, ob_ref, gate_ref, nw_ref, qm_ref, mkv_ref, wo_ref, g1_ref, b1_ref, w1_ref,
                      w2_ref, g2_ref, b2_ref, o_ref, mem_out):
    _mem_attention(qm_ref, mkv_ref, mem_out)
    o = of_ref[0].astype(F32) + ob_ref[0].astype(F32)
    gate = gate_ref[0].astype(F32)
    gate = gate * jax.nn.sigmoid(gate)
    nw = nw_ref[...]
    heads = []
    for h in range(HG_HEADS):
        cs = slice(h * HG_DV, (h + 1) * HG_DV)
        oh = o[:, cs]
        oh = oh * lax.rsqrt(jnp.mean(oh * oh, axis=-1, keepdims=True) + RMS_EPS) * nw
        heads.append((oh * gate[:, cs]).astype(BF16))
    prim = jnp.concatenate(heads, axis=1)
    _mix_ffn_tail(x_ref, prim, mem_out, wo_ref, g1_ref, b1_ref, w1_ref, w2_ref, g2_ref, b2_ref, o_ref)


def _mix_ffn(x, mixer_in, h, qm_col, mem_kv, w_out, ln1_g, ln1_b, w_ff1, w_ff2, ln2_g, ln2_b, norm_w=None):
    bsz, t, d = x.shape
    tm = ROW_TILE
    assert t % tm == 0
    tile = lambda w, col: pl.BlockSpec((1, tm, w), lambda b, i: (b, i, col))
    vec = lambda a: a.reshape(1, -1)
    common_specs = [tile(D_CROSS, qm_col), pl.BlockSpec((1, N_MEM, 2 * D_CROSS), lambda b, i: (b, 0, 0)),
                    _resident(w_out.shape), _resident((1, d)), _resident((1, d)), _resident(w_ff1.shape),
                    _resident(w_ff2.shape), _resident((1, d)), _resident((1, d))]
    common_args = [h, mem_kv, w_out, vec(ln1_g), vec(ln1_b), w_ff1, w_ff2, vec(ln2_g), vec(ln2_b)]
    if norm_w is None:
        body, name = _mix_ffn_a_kernel, "mix_ffn_a"
        specs = [tile(d, 0), tile(D_PRIM, 0)] + common_specs
        args = [x, mixer_in[0]] + common_args
    else:
        body, name = _mix_ffn_b_kernel, "mix_ffn_b"
        specs = [tile(d, 0), tile(D_PRIM, 0), tile(D_PRIM, 0), tile(D_PRIM, 4), _resident((1, HG_DV))] + common_specs
        args = [x, mixer_in[0], mixer_in[1], h, vec(norm_w)] + common_args
    return pl.pallas_call(
        body,
        grid=(bsz, t // tm),
        in_specs=specs,
        out_specs=tile(d, 0),
        out_shape=jax.ShapeDtypeStruct((bsz, t, d), F32),
        scratch_shapes=[pltpu.VMEM((tm, D_CROSS), BF16)],
        compiler_params=_params("parallel", "parallel"),
        name=name,
    )(*args)


def _trunk(x, mem, w_mem_kv, w_in_a, na_bias, w_in_b, lb_logits, hg_norm_w, w_out, ln1_g, ln1_b, w_ff1, w_ff2,
           ln2_g, ln2_b):
    bsz, t, d = x.shape
    rows = t // GRID_W
    mem_kv = _proj(mem.reshape(bsz * N_MEM, d), w_mem_kv, ROW_TILE).reshape(bsz, N_MEM, 2 * D_CROSS)
    for layer in range(DEPTH):
        j = layer // 2
        if layer % 2 == 0:
            h = _proj(x.reshape(bsz * t, d), w_in_a[j], ROW_TILE).reshape(bsz, t, -1)
            mixer_in = (_na_attn(h, na_bias[j], rows),)
            qm_col, norm_w = 3 * D_PRIM // D_CROSS, None
        else:
            h = _proj(x.reshape(bsz * t, d), w_in_b[j], ROW_TILE).reshape(bsz, t, -1)
            mixer_in = _hgrn_scan(h, lb_logits, layer)
            qm_col, norm_w = (3 * HG_F + 2 * D_PRIM) // D_CROSS, hg_norm_w[j]
        x = _mix_ffn(x, mixer_in, h, qm_col, mem_kv, w_out[layer], ln1_g[layer], ln1_b[layer], w_ff1[layer],
                     w_ff2[layer], ln2_g[layer], ln2_b[layer], norm_w)
    return x


def kernel(x_prompt, x_sample, mem_prompt, mem_sample, w_mem_kv, w_in_a, rpb, w_in_b, lb_logits, hg_norm_w, w_out,
           ln1_g, ln1_b, w_ff1, w_ff2, ln2_g, ln2_b):
    bf = lambda a: a.astype(BF16)
    na_bias = jax.vmap(_na_bias_table)(rpb)
    shared = (bf(w_mem_kv), bf(w_in_a), na_bias, bf(w_in_b), lb_logits.astype(F32), hg_norm_w, bf(w_out), ln1_g,
              ln1_b, bf(w_ff1), bf(w_ff2), ln2_g, ln2_b)
    return (_trunk(x_prompt, mem_prompt, *shared), _trunk(x_sample, mem_sample, *shared))
```

```python
import functools

import jax
import jax.numpy as jnp
from jax import lax
from jax.experimental import pallas as pl
from jax.experimental.pallas import tpu as pltpu

F32 = jnp.float32
BF16 = jnp.bfloat16

D_MODEL = 1024
DEPTH = 2
GRID_W = 64
WIN_R = 8
WIN_C = 16
N_MEM = 256
CROSS_HEADS = 4
CROSS_DH = 64
D_CROSS = CROSS_HEADS * CROSS_DH
D_PRIM = D_MODEL - D_CROSS
NA_HEADS = 12
NA_DH = D_PRIM // NA_HEADS
HG_HEADS = 6
HG_DK = 128
HG_DV = D_PRIM // HG_HEADS
HG_F = HG_HEADS * HG_DK
D_FF = 4 * D_MODEL
ALPHA = (2 * DEPTH) ** 0.25
LN_EPS = 1e-5
RMS_EPS = 1e-6

LANES = 128
SUBLANES = 8
VMEM_LIMIT = 56 * 1024 * 1024

ROW_TILE = 512
NA_ROWS = 8
NA_TOK = NA_ROWS * GRID_W
NA_PAIRS = NA_HEADS // 2
HG_CHUNK = 32
HG_SUB = SUBLANES
HG_NSUB = HG_CHUNK // HG_SUB
HG_BLOCK = 256
HG_NCHUNK = HG_BLOCK // HG_CHUNK
FF_CHUNK = 1024
ATT_SUB = 128


def _params(*sem):
    return pltpu.CompilerParams(dimension_semantics=sem, vmem_limit_bytes=VMEM_LIMIT)


def _resident(shape):
    zeros = (0,) * len(shape)
    return pl.BlockSpec(shape, lambda *_: zeros, pipeline_mode=pl.Buffered(1))


def _proj_kernel(x_ref, w_ref, o_ref, *, n_chunk):
    xb = x_ref[...].astype(BF16)
    for c in range(0, o_ref.shape[1], n_chunk):
        acc = jnp.dot(xb, w_ref[:, c:c + n_chunk], preferred_element_type=F32)
        o_ref[:, c:c + n_chunk] = acc.astype(o_ref.dtype)


def _proj(x2d, w, tm):
    n_rows, d = x2d.shape
    n_out = w.shape[1]
    tm = min(tm, n_rows)
    assert n_rows % tm == 0
    n_chunk = 512 if n_out % 512 == 0 else n_out
    return pl.pallas_call(
        functools.partial(_proj_kernel, n_chunk=n_chunk),
        grid=(n_rows // tm,),
        in_specs=[pl.BlockSpec((tm, d), lambda i: (i, 0)), _resident((d, n_out))],
        out_specs=pl.BlockSpec((tm, n_out), lambda i: (i, 0)),
        out_shape=jax.ShapeDtypeStruct((n_rows, n_out), BF16),
        compiler_params=_params("parallel"),
        name="proj",
    )(x2d, w)


def _na_bias_table(rpb):
    qc = jnp.arange(GRID_W)[:, None]
    kc = jnp.arange(GRID_W)[None, :]
    cstart = jnp.clip(qc - WIN_C // 2, 0, GRID_W - WIN_C)
    valid = (kc >= cstart) & (kc < cstart + WIN_C)
    dcol = jnp.clip(kc - qc + WIN_C - 1, 0, 2 * WIN_C - 2)
    t = jnp.where(valid, rpb[:, :, dcol], -jnp.inf)
    nd = 2 * WIN_R - 2
    x = jnp.stack([t[:, :nd], t[:, 1:]], axis=3)
    x = x.reshape(NA_PAIRS, 2, nd, GRID_W, 2 * GRID_W)
    return jnp.transpose(x, (0, 2, 1, 3, 4)).reshape(NA_PAIRS, nd, 2 * GRID_W, 2 * GRID_W).astype(F32)


def _na_kernel(q_ref, kp_ref, kc_ref, kn_ref, vp_ref, vc_ref, vn_ref, bias_ref, o_ref, kcat, vcat, *, rows):
    j = pl.program_id(1)
    kcat[0:NA_TOK] = kp_ref[0]
    kcat[NA_TOK:2 * NA_TOK] = kc_ref[0]
    kcat[2 * NA_TOK:3 * NA_TOK] = kn_ref[0]
    vcat[0:NA_TOK] = vp_ref[0]
    vcat[NA_TOK:2 * NA_TOK] = vc_ref[0]
    vcat[2 * NA_TOK:3 * NA_TOK] = vn_ref[0]

    lo = lax.broadcasted_iota(jnp.int32, (1, LANES), 1) < NA_DH
    n_keys = WIN_R * GRID_W

    def row_body(i, carry):
        r = j * NA_ROWS + i
        rs = jnp.clip(r - WIN_R // 2, 0, rows - WIN_R)
        delta = r - rs
        start = pl.multiple_of((rs - j * NA_ROWS + NA_ROWS) * GRID_W, GRID_W)
        qrow = pl.ds(pl.multiple_of(i * GRID_W, GRID_W), GRID_W)
        for p in range(NA_PAIRS):
            cols = slice(p * LANES, (p + 1) * LANES)
            q = q_ref[0, qrow, cols] * (NA_DH ** -0.5)
            zero = jnp.zeros_like(q)
            q2 = jnp.concatenate([jnp.where(lo, q, zero), jnp.where(lo, zero, q)], axis=0)
            kk = kcat[pl.ds(start, n_keys), cols]
            vv = vcat[pl.ds(start, n_keys), cols]
            s = lax.dot_general(q2, kk, (((1,), (1,)), ((), ())), preferred_element_type=F32)
            bias = jnp.concatenate(
                [bias_ref[p, WIN_R - 1 - delta + 2 * m] for m in range(WIN_R // 2)], axis=1)
            s = s + bias
            mx = jnp.max(s, axis=-1, keepdims=True)
            e = jnp.exp(s - mx)
            den = jnp.sum(e, axis=-1, keepdims=True)
            o = jnp.dot(e.astype(BF16), vv, preferred_element_type=F32) / den
            out = jnp.where(lo, o[0:GRID_W], o[GRID_W:2 * GRID_W])
            o_ref[0, qrow, cols] = out.astype(o_ref.dtype)
        return carry

    lax.fori_loop(0, NA_ROWS, row_body, 0, unroll=2)


def _na_attn(h, bias, rows):
    bsz, t, _ = h.shape
    assert rows % NA_ROWS == 0 and rows >= WIN_R and t == rows * GRID_W
    nb = rows // NA_ROWS
    blk = (1, NA_TOK, D_PRIM)

    def spec(col, shift):
        return pl.BlockSpec(blk, lambda b, j: (b, jnp.clip(j + shift, 0, nb - 1), col))

    return pl.pallas_call(
        functools.partial(_na_kernel, rows=rows),
        grid=(bsz, nb),
        in_specs=[spec(0, 0), spec(1, -1), spec(1, 0), spec(1, 1), spec(2, -1), spec(2, 0), spec(2, 1),
                  _resident(bias.shape)],
        out_specs=pl.BlockSpec(blk, lambda b, j: (b, j, 0)),
        out_shape=jax.ShapeDtypeStruct((bsz, t, D_PRIM), BF16),
        scratch_shapes=[pltpu.VMEM((3 * NA_TOK, D_PRIM), BF16), pltpu.VMEM((3 * NA_TOK, D_PRIM), BF16)],
        compiler_params=_params("parallel", "parallel"),
        name="na_attn",
    )(h, h, h, h, h, h, h, bias)


def _hgrn_consts(rev):
    t = lax.broadcasted_iota(jnp.int32, (HG_BLOCK, HG_BLOCK), 0)
    s = lax.broadcasted_iota(jnp.int32, (HG_BLOCK, HG_BLOCK), 1)
    seen = (s >= t) if rev else (s <= t)
    tri = jnp.where(seen & (s // HG_CHUNK == t // HG_CHUNK), 1.0, 0.0).astype(BF16)
    kr = lax.broadcasted_iota(jnp.int32, (2 * LANES, LANES), 0)
    kl = lax.broadcasted_iota(jnp.int32, (2 * LANES, LANES), 1)
    fold = jnp.where(kr // LANES == kl // HG_CHUNK, 1.0, 0.0).astype(BF16)
    trow = lax.broadcasted_iota(jnp.int32, (HG_CHUNK, LANES), 0)
    lane = lax.broadcasted_iota(jnp.int32, (HG_CHUNK, LANES), 1)
    key = lane & (HG_CHUNK - 1)
    same = (key // HG_SUB == trow // HG_SUB) & (lane < 2 * HG_CHUNK)
    diag_hit = []
    for d in range(0, HG_SUB, 2):
        dist = jnp.where(lane < HG_CHUNK, d, d + 1)
        diag_hit.append(same & (key == (trow + dist if rev else trow - dist)))
    sub_lane = lax.broadcasted_iota(jnp.int32, (HG_SUB, LANES), 1)
    off_cols = [(sub_lane >= i * HG_SUB) & (sub_lane < (i + 1) * HG_SUB) for i in range(HG_NSUB)]
    return tri, fold, diag_hit, off_cols


def _hgrn_head(ql, fl, v, lb, st, consts, rev):
    tri, fold, diag_hit, off_cols = consts
    qs = ql * jax.nn.sigmoid(ql)
    f = lb + (1.0 - lb) * jax.nn.sigmoid(fl)
    k = 1.0 - f
    g = jnp.log(f)
    g_hi = g.astype(BF16)
    g_lo = (g - g_hi.astype(F32)).astype(BF16)
    b = jnp.dot(tri, jnp.concatenate([g_hi, g_lo], axis=1), preferred_element_type=F32)
    b = b[:, :LANES] + b[:, LANES:]
    yield None

    def pos(p):
        return HG_NSUB - 1 - p if rev else p

    def sub(a, c, p):
        r0 = c * HG_CHUNK + pos(p) * HG_SUB
        return a[r0:r0 + HG_SUB]

    def end(c, p):
        r = c * HG_CHUNK + pos(p) * HG_SUB + (0 if rev else HG_SUB - 1)
        return b[r:r + 1]

    pairs = [(i, j) for i in range(1, HG_NSUB) for j in range(i)]
    q_bar, k_hat, k_bar, lhs_off, decay = {}, {}, {}, [], []
    for c in range(HG_NCHUNK):
        ends = [end(c, p) for p in range(HG_NSUB)]
        q_hat = []
        for p in range(HG_NSUB):
            bb = sub(b, c, p)
            q_hat.append(sub(qs, c, p) * jnp.exp(bb if p == 0 else bb - ends[p - 1]))
            kh = sub(k, c, p) * jnp.exp(ends[p] - bb)
            k_hat[c, pos(p)] = kh
            q_bar[c, pos(p)] = q_hat[p] if p == 0 else q_hat[p] * jnp.exp(ends[p - 1])
            k_bar[c, pos(p)] = kh if p == HG_NSUB - 1 else kh * jnp.exp(ends[-1] - ends[p])
        lhs_off += [q_hat[i] if i == j + 1 else q_hat[i] * jnp.exp(ends[i - 1] - ends[j]) for i, j in pairs]
        decay.append(jnp.exp(ends[-1]))

    def rows_cat(parts):
        return jnp.concatenate([parts[c, i] for c in range(HG_NCHUNK) for i in range(HG_NSUB)], axis=0).astype(BF16)

    q_bar, k_hat, k_bar = rows_cat(q_bar), rows_cat(k_hat), rows_cat(k_bar)
    lhs_off = jnp.concatenate(lhs_off, axis=0).astype(BF16)
    vb = v.astype(BF16)

    shift = HG_SUB - 1 if rev else 1

    def prev(a):
        a3 = a.reshape(HG_BLOCK // HG_SUB, HG_SUB, LANES)
        return pltpu.roll(a3, shift, 1).reshape(HG_BLOCK, LANES)

    kd = k
    terms = [qs * k]
    for d in range(1, HG_SUB):
        kd = f * prev(kd)
        terms.append(qs * kd)
    terms = [x.astype(BF16) for x in terms]
    z = jnp.concatenate([jnp.concatenate([terms[d], terms[d + 1]], axis=1) for d in range(0, HG_SUB, 2)], axis=0)

    zpad = jnp.zeros((LANES - HG_CHUNK, LANES), BF16)
    vpad = jnp.zeros((LANES - 2 * HG_CHUNK, LANES), BF16)
    n_off = len(pairs) * HG_SUB
    chunk_rows = [slice(c * HG_CHUNK, (c + 1) * HG_CHUNK) for c in range(HG_NCHUNK)]
    r_offs = [lax.dot_general(lhs_off[c * n_off:(c + 1) * n_off], jnp.concatenate([k_hat[rs], zpad], axis=0),
                              (((1,), (1,)), ((), ())), preferred_element_type=F32)
              for c, rs in enumerate(chunk_rows)]
    rd = jnp.dot(z, fold, preferred_element_type=F32)
    upd = [lax.dot_general(vb[rs], k_bar[rs], (((0,), (0,)), ((), ())), preferred_element_type=F32)
           for rs in chunk_rows]
    yield None

    o_intra = []
    for c, rs in enumerate(chunk_rows):
        r_off = r_offs[c]
        a_rows = []
        for i in range(HG_NSUB):
            p = pos(i)
            acc = jnp.zeros((HG_SUB, LANES), F32)
            for j in range(p):
                n = pairs.index((p, j))
                acc = jnp.where(off_cols[pos(j)], r_off[n * HG_SUB:(n + 1) * HG_SUB], acc)
            a_rows.append(acc)
        a = jnp.concatenate(a_rows, axis=0)
        for n in range(HG_SUB // 2):
            a = a + jnp.where(diag_hit[n], rd[n * HG_BLOCK + c * HG_CHUNK:n * HG_BLOCK + (c + 1) * HG_CHUNK], 0.0)
        v_c = vb[rs]
        v2 = jnp.concatenate([v_c, v_c, vpad], axis=0)
        o_intra.append(jnp.dot(a.astype(BF16), v2, preferred_element_type=F32))
    yield None

    outs = [None] * HG_NCHUNK
    for step in range(HG_NCHUNK):
        c = HG_NCHUNK - 1 - step if rev else step
        outs[c] = o_intra[c] + lax.dot_general(q_bar[chunk_rows[c]], st.astype(BF16), (((1,), (1,)), ((), ())),
                                               preferred_element_type=F32)
        st = st * decay[c] + upd[c]
    yield jnp.concatenate(outs, axis=0), st


def _hgrn_kernel(lbl_ref, qf_ref, ff_ref, vf_ref, qb_ref, fb_ref, vb_ref, of_ref, ob_ref, st_f, st_b, lb_ref, *,
                 layer):
    @pl.when(pl.program_id(1) == 0)
    def _():
        st_f[...] = jnp.zeros_like(st_f)
        st_b[...] = jnp.zeros_like(st_b)

    logits = lbl_ref[...]
    mx = jnp.max(logits, axis=0, keepdims=True)
    ex = jnp.exp(logits - mx)
    prob = ex / jnp.sum(ex, axis=0, keepdims=True)
    lb_ref[...] = jnp.sum(prob[0:layer + 1], axis=0, keepdims=True) - prob[0:1]

    consts_f, consts_b = _hgrn_consts(False), _hgrn_consts(True)

    def head(h, carry):
        cs = pl.ds(pl.multiple_of(h * LANES, LANES), LANES)
        dirs = ((qf_ref, ff_ref, vf_ref, of_ref, st_f, consts_f, False),
                (qb_ref, fb_ref, vb_ref, ob_ref, st_b, consts_b, True))
        gens = [_hgrn_head(q_ref[0, :, cs].astype(F32), f_ref[0, :, cs].astype(F32), v_ref[0, :, cs].astype(F32),
                           lb_ref[:, cs], st_ref[h], consts, rev)
                for q_ref, f_ref, v_ref, _, st_ref, consts, rev in dirs]
        results = [None, None]
        while results[0] is None:
            results = [next(g) for g in gens]
        for (_, _, _, o_ref, st_ref, _, _), (o, st) in zip(dirs, results):
            o_ref[0, :, cs] = o.astype(o_ref.dtype)
            st_ref[h] = st
        return carry

    lax.fori_loop(0, HG_HEADS, head, 0, unroll=2)


def _hgrn_scan(h, lb_logits, layer):
    bsz, t, _ = h.shape
    assert t % HG_BLOCK == 0
    nb = t // HG_BLOCK
    blk = (1, HG_BLOCK, HG_F)
    fwd = lambda col: pl.BlockSpec(blk, lambda b, j: (b, j, col))
    bwd = lambda col: pl.BlockSpec(blk, lambda b, j: (b, nb - 1 - j, col))
    out_sd = jax.ShapeDtypeStruct((bsz, t, D_PRIM), BF16)
    state = pltpu.VMEM((HG_HEADS, HG_DV, HG_DK), F32)
    return pl.pallas_call(
        functools.partial(_hgrn_kernel, layer=layer),
        grid=(bsz, nb),
        in_specs=[_resident(lb_logits.shape), fwd(0), fwd(1), fwd(3), bwd(0), bwd(2), bwd(3)],
        out_specs=[fwd(0), bwd(0)],
        out_shape=[out_sd, out_sd],
        scratch_shapes=[state, state, pltpu.VMEM((1, HG_F), F32)],
        compiler_params=_params("parallel", "arbitrary"),
        name="hgrn_scan",
    )(lb_logits, h, h, h, h, h, h)


def _layer_norm(y, g, b):
    mu = jnp.mean(y, axis=-1, keepdims=True)
    yc = y - mu
    var = jnp.mean(yc * yc, axis=-1, keepdims=True)
    return yc * lax.rsqrt(var + LN_EPS) * g + b


def _mem_attention(qm_ref, mkv_ref, mem_out):
    lo = lax.broadcasted_iota(jnp.int32, (1, LANES), 1) < CROSS_DH

    for t in range(qm_ref.shape[1] // ATT_SUB):
        rws = slice(t * ATT_SUB, (t + 1) * ATT_SUB)
        for p in range(CROSS_HEADS // 2):
            cols = slice(p * LANES, (p + 1) * LANES)
            q = qm_ref[0, rws, cols] * (CROSS_DH ** -0.5)
            zero = jnp.zeros_like(q)
            q2 = jnp.concatenate([jnp.where(lo, q, zero), jnp.where(lo, zero, q)], axis=0)
            mk = mkv_ref[0, :, cols]
            mv = mkv_ref[0, :, D_CROSS + p * LANES:D_CROSS + (p + 1) * LANES]
            s = lax.dot_general(q2, mk, (((1,), (1,)), ((), ())), preferred_element_type=F32)
            mx = jnp.max(s, axis=-1, keepdims=True)
            e = jnp.exp(s - mx)
            den = jnp.sum(e, axis=-1, keepdims=True)
            o = jnp.dot(e.astype(BF16), mv, preferred_element_type=F32) / den
            mem_out[rws, cols] = jnp.where(lo, o[0:ATT_SUB], o[ATT_SUB:2 * ATT_SUB]).astype(BF16)


def _mix_ffn_tail(x_ref, prim, mem_out, wo_ref, g1_ref, b1_ref, w1_ref, w2_ref, g2_ref, b2_ref, o_ref):
    mix = (jnp.dot(prim, wo_ref[0:D_PRIM, :], preferred_element_type=F32)
           + jnp.dot(mem_out[...], wo_ref[D_PRIM:D_MODEL, :], preferred_element_type=F32))
    x1 = _layer_norm(ALPHA * x_ref[0] + mix, g1_ref[...], b1_ref[...])
    x1b = x1.astype(BF16)
    acc = jnp.zeros(x1.shape, F32)
    for c in range(0, D_FF, FF_CHUNK):
        hid = jnp.dot(x1b, w1_ref[:, c:c + FF_CHUNK], preferred_element_type=F32)
        hid = jnp.square(jnp.maximum(hid, 0.0)).astype(BF16)
        acc = acc + jnp.dot(hid, w2_ref[c:c + FF_CHUNK, :], preferred_element_type=F32)
    o_ref[0] = _layer_norm(ALPHA * x1 + acc, g2_ref[...], b2_ref[...])


def _mix_ffn_a_kernel(x_ref, prim_ref, qm_ref, mkv_ref, wo_ref, g1_ref, b1_ref, w1_ref, w2_ref, g2_ref, b2_ref,
                      o_ref, mem_out):
    _mem_attention(qm_ref, mkv_ref, mem_out)
    _mix_ffn_tail(x_ref, prim_ref[0], mem_out, wo_ref, g1_ref, b1_ref, w1_ref, w2_ref, g2_ref, b2_ref, o_ref)


def _mix_ffn_b_kernel(x_ref, of_ref, ob_ref, gate_ref, nw_ref, qm_ref, mkv_ref, wo_ref, g1_ref, b1_ref, w1_ref,
                      w2_ref, g2_ref, b2_ref, o_ref, mem_out):
    _mem_attention(qm_ref, mkv_ref, mem_out)
    o = of_ref[0].astype(F32) + ob_ref[0].astype(F32)
    gate = gate_ref[0].astype(F32)
    gate = gate * jax.nn.sigmoid(gate)
    nw = nw_ref[...]
    heads = []
    for h in range(HG_HEADS):
        cs = slice(h * HG_DV, (h + 1) * HG_DV)
        oh = o[:, cs]
        oh = oh * lax.rsqrt(jnp.mean(oh * oh, axis=-1, keepdims=True) + RMS_EPS) * nw
        heads.append((oh * gate[:, cs]).astype(BF16))
    prim = jnp.concatenate(heads, axis=1)
    _mix_ffn_tail(x_ref, prim, mem_out, wo_ref, g1_ref, b1_ref, w1_ref, w2_ref, g2_ref, b2_ref, o_ref)


def _mix_ffn(x, mixer_in, h, qm_col, mem_kv, w_out, ln1_g, ln1_b, w_ff1, w_ff2, ln2_g, ln2_b, norm_w=None):
    bsz, t, d = x.shape
    tm = ROW_TILE
    assert t % tm == 0
    tile = lambda w, col: pl.BlockSpec((1, tm, w), lambda b, i: (b, i, col))
    vec = lambda a: a.reshape(1, -1)
    common_specs = [tile(D_CROSS, qm_col), pl.BlockSpec((1, N_MEM, 2 * D_CROSS), lambda b, i: (b, 0, 0)),
                    _resident(w_out.shape), _resident((1, d)), _resident((1, d)), _resident(w_ff1.shape),
                    _resident(w_ff2.shape), _resident((1, d)), _resident((1, d))]
    common_args = [h, mem_kv, w_out, vec(ln1_g), vec(ln1_b), w_ff1, w_ff2, vec(ln2_g), vec(ln2_b)]
    if norm_w is None:
        body, name = _mix_ffn_a_kernel, "mix_ffn_a"
        specs = [tile(d, 0), tile(D_PRIM, 0)] + common_specs
        args = [x, mixer_in[0]] + common_args
    else:
        body, name = _mix_ffn_b_kernel, "mix_ffn_b"
        specs = [tile(d, 0), tile(D_PRIM, 0), tile(D_PRIM, 0), tile(D_PRIM, 4), _resident((1, HG_DV))] + common_specs
        args = [x, mixer_in[0], mixer_in[1], h, vec(norm_w)] + common_args
    return pl.pallas_call(
        body,
        grid=(bsz, t // tm),
        in_specs=specs,
        out_specs=tile(d, 0),
        out_shape=jax.ShapeDtypeStruct((bsz, t, d), F32),
        scratch_shapes=[pltpu.VMEM((tm, D_CROSS), BF16)],
        compiler_params=_params("parallel", "parallel"),
        name=name,
    )(*args)


def _trunk(x, mem, w_mem_kv, w_in_a, na_bias, w_in_b, lb_logits, hg_norm_w, w_out, ln1_g, ln1_b, w_ff1, w_ff2,
           ln2_g, ln2_b):
    bsz, t, d = x.shape
    rows = t // GRID_W
    mem_kv = _proj(mem.reshape(bsz * N_MEM, d), w_mem_kv, ROW_TILE).reshape(bsz, N_MEM, 2 * D_CROSS)
    for layer in range(DEPTH):
        j = layer // 2
        if layer % 2 == 0:
            h = _proj(x.reshape(bsz * t, d), w_in_a[j], ROW_TILE).reshape(bsz, t, -1)
            mixer_in = (_na_attn(h, na_bias[j], rows),)
            qm_col, norm_w = 3 * D_PRIM // D_CROSS, None
        else:
            h = _proj(x.reshape(bsz * t, d), w_in_b[j], ROW_TILE).reshape(bsz, t, -1)
            mixer_in = _hgrn_scan(h, lb_logits, layer)
            qm_col, norm_w = (3 * HG_F + 2 * D_PRIM) // D_CROSS, hg_norm_w[j]
        x = _mix_ffn(x, mixer_in, h, qm_col, mem_kv, w_out[layer], ln1_g[layer], ln1_b[layer], w_ff1[layer],
                     w_ff2[layer], ln2_g[layer], ln2_b[layer], norm_w)
    return x


def kernel(x_prompt, x_sample, mem_prompt, mem_sample, w_mem_kv, w_in_a, rpb, w_in_b, lb_logits, hg_norm_w, w_out,
           ln1_g, ln1_b, w_ff1, w_ff2, ln2_g, ln2_b):
    bf = lambda a: a.astype(BF16)
    na_bias = jax.vmap(_na_bias_table)(rpb)
    shared = (bf(w_mem_kv), bf(w_in_a), na_bias, bf(w_in_b), lb_logits.astype(F32), hg_norm_w, bf(w_out), ln1_g,
              ln1_b, bf(w_ff1), bf(w_ff2), ln2_g, ln2_b)
    return (_trunk(x_prompt, mem_prompt, *shared), _trunk(x_sample, mem_sample, *shared))
```

```python
import functools

import jax
import jax.numpy as jnp
from jax import lax
from jax.experimental import pallas as pl
from jax.experimental.pallas import tpu as pltpu

F32 = jnp.float32
BF16 = jnp.bfloat16

D_MODEL = 1024
DEPTH = 2
GRID_W = 64
WIN_R = 8
WIN_C = 16
N_MEM = 256
CROSS_HEADS = 4
CROSS_DH = 64
D_CROSS = CROSS_HEADS * CROSS_DH
D_PRIM = D_MODEL - D_CROSS
NA_HEADS = 12
NA_DH = D_PRIM // NA_HEADS
HG_HEADS = 6
HG_DK = 128
HG_DV = D_PRIM // HG_HEADS
HG_F = HG_HEADS * HG_DK
D_FF = 4 * D_MODEL
ALPHA = (2 * DEPTH) ** 0.25
LN_EPS = 1e-5
RMS_EPS = 1e-6

LANES = 128
SUBLANES = 8
VMEM_LIMIT = 56 * 1024 * 1024

ROW_TILE = 512
NA_ROWS = 8
NA_TOK = NA_ROWS * GRID_W
NA_PAIRS = NA_HEADS // 2
HG_CHUNK = 32
HG_SUB = SUBLANES
HG_NSUB = HG_CHUNK // HG_SUB
HG_BLOCK = 256
HG_NCHUNK = HG_BLOCK // HG_CHUNK
FF_CHUNK = 1024
ATT_SUB = 128


def _params(*sem):
    return pltpu.CompilerParams(dimension_semantics=sem, vmem_limit_bytes=VMEM_LIMIT)


def _resident(shape):
    zeros = (0,) * len(shape)
    return pl.BlockSpec(shape, lambda *_: zeros, pipeline_mode=pl.Buffered(1))


def _proj_kernel(x_ref, w_ref, o_ref, *, n_chunk):
    xb = x_ref[...].astype(BF16)
    for c in range(0, o_ref.shape[1], n_chunk):
        acc = jnp.dot(xb, w_ref[:, c:c + n_chunk], preferred_element_type=F32)
        o_ref[:, c:c + n_chunk] = acc.astype(o_ref.dtype)


def _proj(x2d, w, tm):
    n_rows, d = x2d.shape
    n_out = w.shape[1]
    tm = min(tm, n_rows)
    assert n_rows % tm == 0
    n_chunk = 512 if n_out % 512 == 0 else n_out
    return pl.pallas_call(
        functools.partial(_proj_kernel, n_chunk=n_chunk),
        grid=(n_rows // tm,),
        in_specs=[pl.BlockSpec((tm, d), lambda i: (i, 0)), _resident((d, n_out))],
        out_specs=pl.BlockSpec((tm, n_out), lambda i: (i, 0)),
        out_shape=jax.ShapeDtypeStruct((n_rows, n_out), BF16),
        compiler_params=_params("parallel"),
        name="proj",
    )(x2d, w)


def _na_bias_table(rpb):
    qc = jnp.arange(GRID_W)[:, None]
    kc = jnp.arange(GRID_W)[None, :]
    cstart = jnp.clip(qc - WIN_C // 2, 0, GRID_W - WIN_C)
    valid = (kc >= cstart) & (kc < cstart + WIN_C)
    dcol = jnp.clip(kc - qc + WIN_C - 1, 0, 2 * WIN_C - 2)
    t = jnp.where(valid, rpb[:, :, dcol], -jnp.inf)
    nd = 2 * WIN_R - 2
    x = jnp.stack([t[:, :nd], t[:, 1:]], axis=3)
    x = x.reshape(NA_PAIRS, 2, nd, GRID_W, 2 * GRID_W)
    return jnp.transpose(x, (0, 2, 1, 3, 4)).reshape(NA_PAIRS, nd, 2 * GRID_W, 2 * GRID_W).astype(F32)


def _na_kernel(q_ref, kp_ref, kc_ref, kn_ref, vp_ref, vc_ref, vn_ref, bias_ref, o_ref, kcat, vcat, s_scr, e_scr,
               m_scr, d_scr, *, rows):
    j = pl.program_id(1)
    kcat[0:NA_TOK] = kp_ref[0]
    kcat[NA_TOK:2 * NA_TOK] = kc_ref[0]
    kcat[2 * NA_TOK:3 * NA_TOK] = kn_ref[0]
    vcat[0:NA_TOK] = vp_ref[0]
    vcat[NA_TOK:2 * NA_TOK] = vc_ref[0]
    vcat[2 * NA_TOK:3 * NA_TOK] = vn_ref[0]

    lo = lax.broadcasted_iota(jnp.int32, (1, LANES), 1) < NA_DH
    n_keys = WIN_R * GRID_W

    def window(i):
        r = j * NA_ROWS + i
        rs = jnp.clip(r - WIN_R // 2, 0, rows - WIN_R)
        start = pl.multiple_of((rs - j * NA_ROWS + NA_ROWS) * GRID_W, GRID_W)
        return r - rs, pl.ds(start, n_keys), pl.ds(pl.multiple_of(i * GRID_W, GRID_W), GRID_W)

    rep = (2 * GRID_W, LANES)

    def score_body(i, carry):
        delta, keys, qrow = window(i)
        for p in range(NA_PAIRS):
            cols = slice(p * LANES, (p + 1) * LANES)
            q = q_ref[0, qrow, cols] * (NA_DH ** -0.5)
            zero = jnp.zeros_like(q)
            q2 = jnp.concatenate([jnp.where(lo, q, zero), jnp.where(lo, zero, q)], axis=0)
            s = lax.dot_general(q2, kcat[keys, cols], (((1,), (1,)), ((), ())), preferred_element_type=F32)
            bias = jnp.concatenate(
                [bias_ref[p, WIN_R - 1 - delta + 2 * m] for m in range(WIN_R // 2)], axis=1)
            s = s + bias
            s_scr[i * NA_PAIRS + p] = s
            m_scr[i * NA_PAIRS + p] = jnp.broadcast_to(jnp.max(s, axis=-1, keepdims=True), rep)
        return carry

    lax.fori_loop(0, NA_ROWS, score_body, 0, unroll=2)

    def exp_body(u, carry):
        e = jnp.exp(s_scr[u] - jnp.tile(m_scr[u], (1, n_keys // LANES)))
        d_scr[u] = jnp.broadcast_to(jnp.sum(e, axis=-1, keepdims=True), rep)
        e_scr[u] = e.astype(BF16)
        return carry

    lax.fori_loop(0, NA_ROWS * NA_PAIRS, exp_body, 0, unroll=8)

    def value_body(i, carry):
        _, keys, qrow = window(i)
        for p in range(NA_PAIRS):
            cols = slice(p * LANES, (p + 1) * LANES)
            u = i * NA_PAIRS + p
            o = jnp.dot(e_scr[u], vcat[keys, cols], preferred_element_type=F32) / d_scr[u]
            out = jnp.where(lo, o[0:GRID_W], o[GRID_W:2 * GRID_W])
            o_ref[0, qrow, cols] = out.astype(o_ref.dtype)
        return carry

    lax.fori_loop(0, NA_ROWS, value_body, 0, unroll=2)


def _na_attn(h, bias, rows):
    bsz, t, _ = h.shape
    assert rows % NA_ROWS == 0 and rows >= WIN_R and t == rows * GRID_W
    nb = rows // NA_ROWS
    blk = (1, NA_TOK, D_PRIM)
    n_units = NA_ROWS * NA_PAIRS

    def spec(col, shift):
        return pl.BlockSpec(blk, lambda b, j: (b, jnp.clip(j + shift, 0, nb - 1), col))

    return pl.pallas_call(
        functools.partial(_na_kernel, rows=rows),
        grid=(bsz, nb),
        in_specs=[spec(0, 0), spec(1, -1), spec(1, 0), spec(1, 1), spec(2, -1), spec(2, 0), spec(2, 1),
                  _resident(bias.shape)],
        out_specs=pl.BlockSpec(blk, lambda b, j: (b, j, 0)),
        out_shape=jax.ShapeDtypeStruct((bsz, t, D_PRIM), BF16),
        scratch_shapes=[pltpu.VMEM((3 * NA_TOK, D_PRIM), BF16), pltpu.VMEM((3 * NA_TOK, D_PRIM), BF16),
                        pltpu.VMEM((n_units, 2 * GRID_W, WIN_R * GRID_W), F32),
                        pltpu.VMEM((n_units, 2 * GRID_W, WIN_R * GRID_W), BF16),
                        pltpu.VMEM((n_units, 2 * GRID_W, LANES), F32), pltpu.VMEM((n_units, 2 * GRID_W, LANES), F32)],
        compiler_params=_params("parallel", "parallel"),
        name="na_attn",
    )(h, h, h, h, h, h, h, bias)


def _hgrn_consts(rev):
    t = lax.broadcasted_iota(jnp.int32, (HG_BLOCK, HG_BLOCK), 0)
    s = lax.broadcasted_iota(jnp.int32, (HG_BLOCK, HG_BLOCK), 1)
    seen = (s >= t) if rev else (s <= t)
    tri = jnp.where(seen & (s // HG_CHUNK == t // HG_CHUNK), 1.0, 0.0).astype(BF16)
    kr = lax.broadcasted_iota(jnp.int32, (2 * LANES, LANES), 0)
    kl = lax.broadcasted_iota(jnp.int32, (2 * LANES, LANES), 1)
    fold = jnp.where(kr // LANES == kl // HG_CHUNK, 1.0, 0.0).astype(BF16)
    trow = lax.broadcasted_iota(jnp.int32, (HG_CHUNK, LANES), 0)
    lane = lax.broadcasted_iota(jnp.int32, (HG_CHUNK, LANES), 1)
    key = lane & (HG_CHUNK - 1)
    same = (key // HG_SUB == trow // HG_SUB) & (lane < 2 * HG_CHUNK)
    diag_hit = []
    for d in range(0, HG_SUB, 2):
        dist = jnp.where(lane < HG_CHUNK, d, d + 1)
        diag_hit.append(same & (key == (trow + dist if rev else trow - dist)))
    sub_lane = lax.broadcasted_iota(jnp.int32, (HG_SUB, LANES), 1)
    off_cols = [(sub_lane >= i * HG_SUB) & (sub_lane < (i + 1) * HG_SUB) for i in range(HG_NSUB)]
    return tri, fold, diag_hit, off_cols


def _hgrn_head(ql, fl, v, lb, st, consts, rev):
    tri, fold, diag_hit, off_cols = consts
    qs = ql * jax.nn.sigmoid(ql)
    f = lb + (1.0 - lb) * jax.nn.sigmoid(fl)
    k = 1.0 - f
    g = jnp.log(f)
    g_hi = g.astype(BF16)
    g_lo = (g - g_hi.astype(F32)).astype(BF16)
    b = jnp.dot(tri, jnp.concatenate([g_hi, g_lo], axis=1), preferred_element_type=F32)
    b = b[:, :LANES] + b[:, LANES:]
    yield None

    def pos(p):
        return HG_NSUB - 1 - p if rev else p

    def sub(a, c, p):
        r0 = c * HG_CHUNK + pos(p) * HG_SUB
        return a[r0:r0 + HG_SUB]

    def end(c, p):
        r = c * HG_CHUNK + pos(p) * HG_SUB + (0 if rev else HG_SUB - 1)
        return b[r:r + 1]

    pairs = [(i, j) for i in range(1, HG_NSUB) for j in range(i)]
    q_bar, k_hat, k_bar, lhs_off, decay = {}, {}, {}, [], []
    for c in range(HG_NCHUNK):
        ends = [end(c, p) for p in range(HG_NSUB)]
        q_hat = []
        for p in range(HG_NSUB):
            bb = sub(b, c, p)
            q_hat.append(sub(qs, c, p) * jnp.exp(bb if p == 0 else bb - ends[p - 1]))
            kh = sub(k, c, p) * jnp.exp(ends[p] - bb)
            k_hat[c, pos(p)] = kh
            q_bar[c, pos(p)] = q_hat[p] if p == 0 else q_hat[p] * jnp.exp(ends[p - 1])
            k_bar[c, pos(p)] = kh if p == HG_NSUB - 1 else kh * jnp.exp(ends[-1] - ends[p])
        lhs_off += [q_hat[i] if i == j + 1 else q_hat[i] * jnp.exp(ends[i - 1] - ends[j]) for i, j in pairs]
        decay.append(jnp.exp(ends[-1]))

    def rows_cat(parts):
        return jnp.concatenate([parts[c, i] for c in range(HG_NCHUNK) for i in range(HG_NSUB)], axis=0).astype(BF16)

    q_bar, k_hat, k_bar = rows_cat(q_bar), rows_cat(k_hat), rows_cat(k_bar)
    lhs_off = jnp.concatenate(lhs_off, axis=0).astype(BF16)
    vb = v.astype(BF16)

    shift = HG_SUB - 1 if rev else 1

    def prev(a):
        a3 = a.reshape(HG_BLOCK // HG_SUB, HG_SUB, LANES)
        return pltpu.roll(a3, shift, 1).reshape(HG_BLOCK, LANES)

    kd = k
    terms = [qs * k]
    for d in range(1, HG_SUB):
        kd = f * prev(kd)
        terms.append(qs * kd)
    terms = [x.astype(BF16) for x in terms]
    z = jnp.concatenate([jnp.concatenate([terms[d], terms[d + 1]], axis=1) for d in range(0, HG_SUB, 2)], axis=0)

    zpad = jnp.zeros((LANES - HG_CHUNK, LANES), BF16)
    vpad = jnp.zeros((LANES - 2 * HG_CHUNK, LANES), BF16)
    n_off = len(pairs) * HG_SUB
    chunk_rows = [slice(c * HG_CHUNK, (c + 1) * HG_CHUNK) for c in range(HG_NCHUNK)]
    r_offs = [lax.dot_general(lhs_off[c * n_off:(c + 1) * n_off], jnp.concatenate([k_hat[rs], zpad], axis=0),
                              (((1,), (1,)), ((), ())), preferred_element_type=F32)
              for c, rs in enumerate(chunk_rows)]
    rd = jnp.dot(z, fold, preferred_element_type=F32)
    upd = [lax.dot_general(vb[rs], k_bar[rs], (((0,), (0,)), ((), ())), preferred_element_type=F32)
           for rs in chunk_rows]
    yield None

    o_intra = []
    for c, rs in enumerate(chunk_rows):
        r_off = r_offs[c]
        a_rows = []
        for i in range(HG_NSUB):
            p = pos(i)
            acc = jnp.zeros((HG_SUB, LANES), F32)
            for j in range(p):
                n = pairs.index((p, j))
                acc = jnp.where(off_cols[pos(j)], r_off[n * HG_SUB:(n + 1) * HG_SUB], acc)
            a_rows.append(acc)
        a = jnp.concatenate(a_rows, axis=0)
        for n in range(HG_SUB // 2):
            a = a + jnp.where(diag_hit[n], rd[n * HG_BLOCK + c * HG_CHUNK:n * HG_BLOCK + (c + 1) * HG_CHUNK], 0.0)
        v_c = vb[rs]
        v2 = jnp.concatenate([v_c, v_c, vpad], axis=0)
        o_intra.append(jnp.dot(a.astype(BF16), v2, preferred_element_type=F32))
    yield None

    outs = [None] * HG_NCHUNK
    for step in range(HG_NCHUNK):
        c = HG_NCHUNK - 1 - step if rev else step
        outs[c] = o_intra[c] + lax.dot_general(q_bar[chunk_rows[c]], st.astype(BF16), (((1,), (1,)), ((), ())),
                                               preferred_element_type=F32)
        st = st * decay[c] + upd[c]
    yield jnp.concatenate(outs, axis=0), st


def _hgrn_kernel(lbl_ref, qf_ref, ff_ref, vf_ref, qb_ref, fb_ref, vb_ref, of_ref, ob_ref, st_f, st_b, lb_ref, *,
                 layer):
    @pl.when(pl.program_id(1) == 0)
    def _():
        st_f[...] = jnp.zeros_like(st_f)
        st_b[...] = jnp.zeros_like(st_b)

    logits = lbl_ref[...]
    mx = jnp.max(logits, axis=0, keepdims=True)
    ex = jnp.exp(logits - mx)
    prob = ex / jnp.sum(ex, axis=0, keepdims=True)
    lb_ref[...] = jnp.sum(prob[0:layer + 1], axis=0, keepdims=True) - prob[0:1]

    consts_f, consts_b = _hgrn_consts(False), _hgrn_consts(True)

    def head(h, carry):
        cs = pl.ds(pl.multiple_of(h * LANES, LANES), LANES)
        dirs = ((qf_ref, ff_ref, vf_ref, of_ref, st_f, consts_f, False),
                (qb_ref, fb_ref, vb_ref, ob_ref, st_b, consts_b, True))
        gens = [_hgrn_head(q_ref[0, :, cs].astype(F32), f_ref[0, :, cs].astype(F32), v_ref[0, :, cs].astype(F32),
                           lb_ref[:, cs], st_ref[h], consts, rev)
                for q_ref, f_ref, v_ref, _, st_ref, consts, rev in dirs]
        results = [None, None]
        while results[0] is None:
            results = [next(g) for g in gens]
        for (_, _, _, o_ref, st_ref, _, _), (o, st) in zip(dirs, results):
            o_ref[0, :, cs] = o.astype(o_ref.dtype)
            st_ref[h] = st
        return carry

    lax.fori_loop(0, HG_HEADS, head, 0, unroll=2)


def _hgrn_scan(h, lb_logits, layer):
    bsz, t, _ = h.shape
    assert t % HG_BLOCK == 0
    nb = t // HG_BLOCK
    blk = (1, HG_BLOCK, HG_F)
    fwd = lambda col: pl.BlockSpec(blk, lambda b, j: (b, j, col))
    bwd = lambda col: pl.BlockSpec(blk, lambda b, j: (b, nb - 1 - j, col))
    out_sd = jax.ShapeDtypeStruct((bsz, t, D_PRIM), BF16)
    state = pltpu.VMEM((HG_HEADS, HG_DV, HG_DK), F32)
    return pl.pallas_call(
        functools.partial(_hgrn_kernel, layer=layer),
        grid=(bsz, nb),
        in_specs=[_resident(lb_logits.shape), fwd(0), fwd(1), fwd(3), bwd(0), bwd(2), bwd(3)],
        out_specs=[fwd(0), bwd(0)],
        out_shape=[out_sd, out_sd],
        scratch_shapes=[state, state, pltpu.VMEM((1, HG_F), F32)],
        compiler_params=_params("parallel", "arbitrary"),
        name="hgrn_scan",
    )(lb_logits, h, h, h, h, h, h)


def _layer_norm(y, g, b):
    mu = jnp.mean(y, axis=-1, keepdims=True)
    yc = y - mu
    var = jnp.mean(yc * yc, axis=-1, keepdims=True)
    return yc * lax.rsqrt(var + LN_EPS) * g + b


def _mem_attention(qm_ref, mkv_ref, mem_out):
    lo = lax.broadcasted_iota(jnp.int32, (1, LANES), 1) < CROSS_DH

    for t in range(qm_ref.shape[1] // ATT_SUB):
        rws = slice(t * ATT_SUB, (t + 1) * ATT_SUB)
        for p in range(CROSS_HEADS // 2):
            cols = slice(p * LANES, (p + 1) * LANES)
            q = qm_ref[0, rws, cols] * (CROSS_DH ** -0.5)
            zero = jnp.zeros_like(q)
            q2 = jnp.concatenate([jnp.where(lo, q, zero), jnp.where(lo, zero, q)], axis=0)
            mk = mkv_ref[0, :, cols]
            mv = mkv_ref[0, :, D_CROSS + p * LANES:D_CROSS + (p + 1) * LANES]
            s = lax.dot_general(q2, mk, (((1,), (1,)), ((), ())), preferred_element_type=F32)
            mx = jnp.max(s, axis=-1, keepdims=True)
            e = jnp.exp(s - mx)
            den = jnp.sum(e, axis=-1, keepdims=True)
            o = jnp.dot(e.astype(BF16), mv, preferred_element_type=F32) / den
            mem_out[rws, cols] = jnp.where(lo, o[0:ATT_SUB], o[ATT_SUB:2 * ATT_SUB]).astype(BF16)


def _mix_ffn_tail(x_ref, prim, mem_out, wo_ref, g1_ref, b1_ref, w1_ref, w2_ref, g2_ref, b2_ref, o_ref):
    mix = (jnp.dot(prim, wo_ref[0:D_PRIM, :], preferred_element_type=F32)
           + jnp.dot(mem_out[...], wo_ref[D_PRIM:D_MODEL, :], preferred_element_type=F32))
    x1 = _layer_norm(ALPHA * x_ref[0] + mix, g1_ref[...], b1_ref[...])
    x1b = x1.astype(BF16)
    acc = jnp.zeros(x1.shape, F32)
    for c in range(0, D_FF, FF_CHUNK):
        hid = jnp.dot(x1b, w1_ref[:, c:c + FF_CHUNK], preferred_element_type=F32)
        hid = jnp.square(jnp.maximum(hid, 0.0)).astype(BF16)
        acc = acc + jnp.dot(hid, w2_ref[c:c + FF_CHUNK, :], preferred_element_type=F32)
    o_ref[0] = _layer_norm(ALPHA * x1 + acc, g2_ref[...], b2_ref[...])


def _mix_ffn_a_kernel(x_ref, prim_ref, qm_ref, mkv_ref, wo_ref, g1_ref, b1_ref, w1_ref, w2_ref, g2_ref, b2_ref,
                      o_ref, mem_out):
    _mem_attention(qm_ref, mkv_ref, mem_out)
    _mix_ffn_tail(x_ref, prim_ref[0], mem_out, wo_ref, g1_ref, b1_ref, w1_ref, w2_ref, g2_ref, b2_ref, o_ref)


def _mix_ffn_b_kernel(x_ref, of_ref, ob_ref, gate_ref, nw_ref, qm_ref, mkv_ref, wo_ref, g1_ref, b1_ref, w1_ref,
                      w2_ref, g2_ref, b2_ref, o_ref, mem_out):
    _mem_attention(qm_ref, mkv_ref, mem_out)
    o = of_ref[0].astype(F32) + ob_ref[0].astype(F32)
    gate = gate_ref[0].astype(F32)
    gate = gate * jax.nn.sigmoid(gate)
    nw = nw_ref[...]
    heads = []
    for h in range(HG_HEADS):
        cs = slice(h * HG_DV, (h + 1) * HG_DV)
        oh = o[:, cs]
        oh = oh * lax.rsqrt(jnp.mean(oh * oh, axis=-1, keepdims=True) + RMS_EPS) * nw
        heads.append((oh * gate[:, cs]).astype(BF16))
    prim = jnp.concatenate(heads, axis=1)
    _mix_ffn_tail(x_ref, prim, mem_out, wo_ref, g1_ref, b1_ref, w1_ref, w2_ref, g2_ref, b2_ref, o_ref)


def _mix_ffn(x, mixer_in, h, qm_col, mem_kv, w_out, ln1_g, ln1_b, w_ff1, w_ff2, ln2_g, ln2_b, norm_w=None):
    bsz, t, d = x.shape
    tm = ROW_TILE
    assert t % tm == 0
    tile = lambda w, col: pl.BlockSpec((1, tm, w), lambda b, i: (b, i, col))
    vec = lambda a: a.reshape(1, -1)
    common_specs = [tile(D_CROSS, qm_col), pl.BlockSpec((1, N_MEM, 2 * D_CROSS), lambda b, i: (b, 0, 0)),
                    _resident(w_out.shape), _resident((1, d)), _resident((1, d)), _resident(w_ff1.shape),
                    _resident(w_ff2.shape), _resident((1, d)), _resident((1, d))]
    common_args = [h, mem_kv, w_out, vec(ln1_g), vec(ln1_b), w_ff1, w_ff2, vec(ln2_g), vec(ln2_b)]
    if norm_w is None:
        body, name = _mix_ffn_a_kernel, "mix_ffn_a"
        specs = [tile(d, 0), tile(D_PRIM, 0)] + common_specs
        args = [x, mixer_in[0]] + common_args
    else:
        body, name = _mix_ffn_b_kernel, "mix_ffn_b"
        specs = [tile(d, 0), tile(D_PRIM, 0), tile(D_PRIM, 0), tile(D_PRIM, 4), _resident((1, HG_DV))] + common_specs
        args = [x, mixer_in[0], mixer_in[1], h, vec(norm_w)] + common_args
    return pl.pallas_call(
        body,
        grid=(bsz, t // tm),
        in_specs=specs,
        out_specs=tile(d, 0),
        out_shape=jax.ShapeDtypeStruct((bsz, t, d), F32),
        scratch_shapes=[pltpu.VMEM((tm, D_CROSS), BF16)],
        compiler_params=_params("parallel", "parallel"),
        name=name,
    )(*args)


def _trunk(x, mem, w_mem_kv, w_in_a, na_bias, w_in_b, lb_logits, hg_norm_w, w_out, ln1_g, ln1_b, w_ff1, w_ff2,
           ln2_g, ln2_b):
    bsz, t, d = x.shape
    rows = t // GRID_W
    mem_kv = _proj(mem.reshape(bsz * N_MEM, d), w_mem_kv, ROW_TILE).reshape(bsz, N_MEM, 2 * D_CROSS)
    for layer in range(DEPTH):
        j = layer // 2
        if layer % 2 == 0:
            h = _proj(x.reshape(bsz * t, d), w_in_a[j], ROW_TILE).reshape(bsz, t, -1)
            mixer_in = (_na_attn(h, na_bias[j], rows),)
            qm_col, norm_w = 3 * D_PRIM // D_CROSS, None
        else:
            h = _proj(x.reshape(bsz * t, d), w_in_b[j], ROW_TILE).reshape(bsz, t, -1)
            mixer_in = _hgrn_scan(h, lb_logits, layer)
            qm_col, norm_w = (3 * HG_F + 2 * D_PRIM) // D_CROSS, hg_norm_w[j]
        x = _mix_ffn(x, mixer_in, h, qm_col, mem_kv, w_out[layer], ln1_g[layer], ln1_b[layer], w_ff1[layer],
                     w_ff2[layer], ln2_g[layer], ln2_b[layer], norm_w)
    return x


def kernel(x_prompt, x_sample, mem_prompt, mem_sample, w_mem_kv, w_in_a, rpb, w_in_b, lb_logits, hg_norm_w, w_out,
           ln1_g, ln1_b, w_ff1, w_ff2, ln2_g, ln2_b):
    bf = lambda a: a.astype(BF16)
    na_bias = jax.vmap(_na_bias_table)(rpb)
    shared = (bf(w_mem_kv), bf(w_in_a), na_bias, bf(w_in_b), lb_logits.astype(F32), hg_norm_w, bf(w_out), ln1_g,
              ln1_b, bf(w_ff1), bf(w_ff2), ln2_g, ln2_b)
    return (_trunk(x_prompt, mem_prompt, *shared), _trunk(x_sample, mem_sample, *shared))
```

```python
import functools

import jax
import jax.numpy as jnp
from jax import lax
from jax.experimental import pallas as pl
from jax.experimental.pallas import tpu as pltpu

F32 = jnp.float32
BF16 = jnp.bfloat16

D_MODEL = 1024
DEPTH = 2
GRID_W = 64
WIN_R = 8
WIN_C = 16
N_MEM = 256
CROSS_HEADS = 4
CROSS_DH = 64
D_CROSS = CROSS_HEADS * CROSS_DH
D_PRIM = D_MODEL - D_CROSS
NA_HEADS = 12
NA_DH = D_PRIM // NA_HEADS
HG_HEADS = 6
HG_DK = 128
HG_DV = D_PRIM // HG_HEADS
HG_F = HG_HEADS * HG_DK
D_FF = 4 * D_MODEL
ALPHA = (2 * DEPTH) ** 0.25
LN_EPS = 1e-5
RMS_EPS = 1e-6

LANES = 128
SUBLANES = 8
VMEM_LIMIT = 56 * 1024 * 1024

ROW_TILE = 512
NA_ROWS = 8
NA_TOK = NA_ROWS * GRID_W
NA_PAIRS = NA_HEADS // 2
HG_CHUNK = 32
HG_SUB = SUBLANES
HG_NSUB = HG_CHUNK // HG_SUB
HG_BLOCK = 256
HG_NCHUNK = HG_BLOCK // HG_CHUNK
FF_CHUNK = 1024
ATT_SUB = 128


def _params(*sem):
    return pltpu.CompilerParams(dimension_semantics=sem, vmem_limit_bytes=VMEM_LIMIT)


def _resident(shape):
    zeros = (0,) * len(shape)
    return pl.BlockSpec(shape, lambda *_: zeros, pipeline_mode=pl.Buffered(1))


def _proj_kernel(x_ref, w_ref, o_ref, *, n_chunk):
    xb = x_ref[...].astype(BF16)
    for c in range(0, o_ref.shape[1], n_chunk):
        acc = jnp.dot(xb, w_ref[:, c:c + n_chunk], preferred_element_type=F32)
        o_ref[:, c:c + n_chunk] = acc.astype(o_ref.dtype)


def _proj(x2d, w, tm):
    n_rows, d = x2d.shape
    n_out = w.shape[1]
    tm = min(tm, n_rows)
    assert n_rows % tm == 0
    n_chunk = 512 if n_out % 512 == 0 else n_out
    return pl.pallas_call(
        functools.partial(_proj_kernel, n_chunk=n_chunk),
        grid=(n_rows // tm,),
        in_specs=[pl.BlockSpec((tm, d), lambda i: (i, 0)), _resident((d, n_out))],
        out_specs=pl.BlockSpec((tm, n_out), lambda i: (i, 0)),
        out_shape=jax.ShapeDtypeStruct((n_rows, n_out), BF16),
        compiler_params=_params("parallel"),
        name="proj",
    )(x2d, w)


def _na_bias_table(rpb):
    qc = jnp.arange(GRID_W)[:, None]
    kc = jnp.arange(GRID_W)[None, :]
    cstart = jnp.clip(qc - WIN_C // 2, 0, GRID_W - WIN_C)
    valid = (kc >= cstart) & (kc < cstart + WIN_C)
    dcol = jnp.clip(kc - qc + WIN_C - 1, 0, 2 * WIN_C - 2)
    t = jnp.where(valid, rpb[:, :, dcol], -jnp.inf)
    nd = 2 * WIN_R - 2
    x = jnp.stack([t[:, :nd], t[:, 1:]], axis=3)
    x = x.reshape(NA_PAIRS, 2, nd, GRID_W, 2 * GRID_W)
    return jnp.transpose(x, (0, 2, 1, 3, 4)).reshape(NA_PAIRS, nd, 2 * GRID_W, 2 * GRID_W).astype(F32)


def _na_kernel(q_ref, kp_ref, kc_ref, kn_ref, vp_ref, vc_ref, vn_ref, bias_ref, o_ref, kcat, vcat, s_scr, e_scr,
               m_scr, d_scr, *, rows):
    j = pl.program_id(1)
    kcat[0:NA_TOK] = kp_ref[0]
    kcat[NA_TOK:2 * NA_TOK] = kc_ref[0]
    kcat[2 * NA_TOK:3 * NA_TOK] = kn_ref[0]
    vcat[0:NA_TOK] = vp_ref[0]
    vcat[NA_TOK:2 * NA_TOK] = vc_ref[0]
    vcat[2 * NA_TOK:3 * NA_TOK] = vn_ref[0]

    lo = lax.broadcasted_iota(jnp.int32, (1, LANES), 1) < NA_DH
    n_keys = WIN_R * GRID_W

    def window(i):
        r = j * NA_ROWS + i
        rs = jnp.clip(r - WIN_R // 2, 0, rows - WIN_R)
        start = pl.multiple_of((rs - j * NA_ROWS + NA_ROWS) * GRID_W, GRID_W)
        return r - rs, pl.ds(start, n_keys), pl.ds(pl.multiple_of(i * GRID_W, GRID_W), GRID_W)

    rep = (2 * GRID_W, LANES)

    def score_body(i, carry):
        delta, keys, qrow = window(i)
        for p in range(NA_PAIRS):
            cols = slice(p * LANES, (p + 1) * LANES)
            q = q_ref[0, qrow, cols] * (NA_DH ** -0.5)
            zero = jnp.zeros_like(q)
            q2 = jnp.concatenate([jnp.where(lo, q, zero), jnp.where(lo, zero, q)], axis=0)
            s = lax.dot_general(q2, kcat[keys, cols], (((1,), (1,)), ((), ())), preferred_element_type=F32)
            bias = jnp.concatenate(
                [bias_ref[p, WIN_R - 1 - delta + 2 * m] for m in range(WIN_R // 2)], axis=1)
            s = s + bias
            s_scr[i * NA_PAIRS + p] = s
            m_scr[i * NA_PAIRS + p] = jnp.broadcast_to(jnp.max(s, axis=-1, keepdims=True), rep)
        return carry

    lax.fori_loop(0, NA_ROWS, score_body, 0, unroll=4)

    def exp_body(u, carry):
        e = jnp.exp(s_scr[u] - jnp.tile(m_scr[u], (1, n_keys // LANES)))
        d_scr[u] = jnp.broadcast_to(jnp.sum(e, axis=-1, keepdims=True), rep)
        e_scr[u] = e.astype(BF16)
        return carry

    lax.fori_loop(0, NA_ROWS * NA_PAIRS, exp_body, 0, unroll=16)

    def value_body(i, carry):
        _, keys, qrow = window(i)
        for p in range(NA_PAIRS):
            cols = slice(p * LANES, (p + 1) * LANES)
            u = i * NA_PAIRS + p
            o = jnp.dot(e_scr[u], vcat[keys, cols], preferred_element_type=F32) / d_scr[u]
            out = jnp.where(lo, o[0:GRID_W], o[GRID_W:2 * GRID_W])
            o_ref[0, qrow, cols] = out.astype(o_ref.dtype)
        return carry

    lax.fori_loop(0, NA_ROWS, value_body, 0, unroll=4)


def _na_attn(h, bias, rows):
    bsz, t, _ = h.shape
    assert rows % NA_ROWS == 0 and rows >= WIN_R and t == rows * GRID_W
    nb = rows // NA_ROWS
    blk = (1, NA_TOK, D_PRIM)
    n_units = NA_ROWS * NA_PAIRS

    def spec(col, shift):
        return pl.BlockSpec(blk, lambda b, j: (b, jnp.clip(j + shift, 0, nb - 1), col))

    return pl.pallas_call(
        functools.partial(_na_kernel, rows=rows),
        grid=(bsz, nb),
        in_specs=[spec(0, 0), spec(1, -1), spec(1, 0), spec(1, 1), spec(2, -1), spec(2, 0), spec(2, 1),
                  _resident(bias.shape)],
        out_specs=pl.BlockSpec(blk, lambda b, j: (b, j, 0)),
        out_shape=jax.ShapeDtypeStruct((bsz, t, D_PRIM), BF16),
        scratch_shapes=[pltpu.VMEM((3 * NA_TOK, D_PRIM), BF16), pltpu.VMEM((3 * NA_TOK, D_PRIM), BF16),
                        pltpu.VMEM((n_units, 2 * GRID_W, WIN_R * GRID_W), F32),
                        pltpu.VMEM((n_units, 2 * GRID_W, WIN_R * GRID_W), BF16),
                        pltpu.VMEM((n_units, 2 * GRID_W, LANES), F32), pltpu.VMEM((n_units, 2 * GRID_W, LANES), F32)],
        compiler_params=_params("parallel", "parallel"),
        name="na_attn",
    )(h, h, h, h, h, h, h, bias)


def _hgrn_consts(rev):
    t = lax.broadcasted_iota(jnp.int32, (HG_BLOCK, HG_BLOCK), 0)
    s = lax.broadcasted_iota(jnp.int32, (HG_BLOCK, HG_BLOCK), 1)
    seen = (s >= t) if rev else (s <= t)
    tri = jnp.where(seen & (s // HG_CHUNK == t // HG_CHUNK), 1.0, 0.0).astype(BF16)
    kr = lax.broadcasted_iota(jnp.int32, (2 * LANES, LANES), 0)
    kl = lax.broadcasted_iota(jnp.int32, (2 * LANES, LANES), 1)
    fold = jnp.where(kr // LANES == kl // HG_CHUNK, 1.0, 0.0).astype(BF16)
    trow = lax.broadcasted_iota(jnp.int32, (HG_CHUNK, LANES), 0)
    lane = lax.broadcasted_iota(jnp.int32, (HG_CHUNK, LANES), 1)
    key = lane & (HG_CHUNK - 1)
    same = (key // HG_SUB == trow // HG_SUB) & (lane < 2 * HG_CHUNK)
    diag_hit = []
    for d in range(0, HG_SUB, 2):
        dist = jnp.where(lane < HG_CHUNK, d, d + 1)
        diag_hit.append(same & (key == (trow + dist if rev else trow - dist)))
    sub_lane = lax.broadcasted_iota(jnp.int32, (HG_SUB, LANES), 1)
    off_cols = [(sub_lane >= i * HG_SUB) & (sub_lane < (i + 1) * HG_SUB) for i in range(HG_NSUB)]
    return tri, fold, diag_hit, off_cols


def _hgrn_head(ql, fl, v, lb, st, consts, rev):
    tri, fold, diag_hit, off_cols = consts
    qs = ql * jax.nn.sigmoid(ql)
    f = lb + (1.0 - lb) * jax.nn.sigmoid(fl)
    k = 1.0 - f
    g = jnp.log(f)
    g_hi = g.astype(BF16)
    g_lo = (g - g_hi.astype(F32)).astype(BF16)
    b = jnp.dot(tri, jnp.concatenate([g_hi, g_lo], axis=1), preferred_element_type=F32)
    b = b[:, :LANES] + b[:, LANES:]
    yield None

    def pos(p):
        return HG_NSUB - 1 - p if rev else p

    def sub(a, c, p):
        r0 = c * HG_CHUNK + pos(p) * HG_SUB
        return a[r0:r0 + HG_SUB]

    def end(c, p):
        r = c * HG_CHUNK + pos(p) * HG_SUB + (0 if rev else HG_SUB - 1)
        return b[r:r + 1]

    pairs = [(i, j) for i in range(1, HG_NSUB) for j in range(i)]
    q_bar, k_hat, k_bar, lhs_off, decay = {}, {}, {}, [], []
    for c in range(HG_NCHUNK):
        ends = [end(c, p) for p in range(HG_NSUB)]
        q_hat = []
        for p in range(HG_NSUB):
            bb = sub(b, c, p)
            q_hat.append(sub(qs, c, p) * jnp.exp(bb if p == 0 else bb - ends[p - 1]))
            kh = sub(k, c, p) * jnp.exp(ends[p] - bb)
            k_hat[c, pos(p)] = kh
            q_bar[c, pos(p)] = q_hat[p] if p == 0 else q_hat[p] * jnp.exp(ends[p - 1])
            k_bar[c, pos(p)] = kh if p == HG_NSUB - 1 else kh * jnp.exp(ends[-1] - ends[p])
        lhs_off += [q_hat[i] if i == j + 1 else q_hat[i] * jnp.exp(ends[i - 1] - ends[j]) for i, j in pairs]
        decay.append(jnp.exp(ends[-1]))

    def rows_cat(parts):
        return jnp.concatenate([parts[c, i] for c in range(HG_NCHUNK) for i in range(HG_NSUB)], axis=0).astype(BF16)

    q_bar, k_hat, k_bar = rows_cat(q_bar), rows_cat(k_hat), rows_cat(k_bar)
    lhs_off = jnp.concatenate(lhs_off, axis=0).astype(BF16)
    vb = v.astype(BF16)

    shift = HG_SUB - 1 if rev else 1

    def prev(a):
        a3 = a.reshape(HG_BLOCK // HG_SUB, HG_SUB, LANES)
        return pltpu.roll(a3, shift, 1).reshape(HG_BLOCK, LANES)

    kd = k
    terms = [qs * k]
    for d in range(1, HG_SUB):
        kd = f * prev(kd)
        terms.append(qs * kd)
    terms = [x.astype(BF16) for x in terms]
    z = jnp.concatenate([jnp.concatenate([terms[d], terms[d + 1]], axis=1) for d in range(0, HG_SUB, 2)], axis=0)

    zpad = jnp.zeros((LANES - HG_CHUNK, LANES), BF16)
    vpad = jnp.zeros((LANES - 2 * HG_CHUNK, LANES), BF16)
    n_off = len(pairs) * HG_SUB
    chunk_rows = [slice(c * HG_CHUNK, (c + 1) * HG_CHUNK) for c in range(HG_NCHUNK)]
    r_offs = [lax.dot_general(lhs_off[c * n_off:(c + 1) * n_off], jnp.concatenate([k_hat[rs], zpad], axis=0),
                              (((1,), (1,)), ((), ())), preferred_element_type=F32)
              for c, rs in enumerate(chunk_rows)]
    rd = jnp.dot(z, fold, preferred_element_type=F32)
    upd = [lax.dot_general(vb[rs], k_bar[rs], (((0,), (0,)), ((), ())), preferred_element_type=F32)
           for rs in chunk_rows]
    yield None

    o_intra = []
    for c, rs in enumerate(chunk_rows):
        r_off = r_offs[c]
        a_rows = []
        for i in range(HG_NSUB):
            p = pos(i)
            acc = jnp.zeros((HG_SUB, LANES), F32)
            for j in range(p):
                n = pairs.index((p, j))
                acc = jnp.where(off_cols[pos(j)], r_off[n * HG_SUB:(n + 1) * HG_SUB], acc)
            a_rows.append(acc)
        a = jnp.concatenate(a_rows, axis=0)
        for n in range(HG_SUB // 2):
            a = a + jnp.where(diag_hit[n], rd[n * HG_BLOCK + c * HG_CHUNK:n * HG_BLOCK + (c + 1) * HG_CHUNK], 0.0)
        v_c = vb[rs]
        v2 = jnp.concatenate([v_c, v_c, vpad], axis=0)
        o_intra.append(jnp.dot(a.astype(BF16), v2, preferred_element_type=F32))
    yield None

    outs = [None] * HG_NCHUNK
    for step in range(HG_NCHUNK):
        c = HG_NCHUNK - 1 - step if rev else step
        outs[c] = o_intra[c] + lax.dot_general(q_bar[chunk_rows[c]], st.astype(BF16), (((1,), (1,)), ((), ())),
                                               preferred_element_type=F32)
        st = st * decay[c] + upd[c]
    yield jnp.concatenate(outs, axis=0), st


def _hgrn_kernel(lbl_ref, qf_ref, ff_ref, vf_ref, qb_ref, fb_ref, vb_ref, of_ref, ob_ref, st_f, st_b, lb_ref, *,
                 layer):
    @pl.when(pl.program_id(1) == 0)
    def _():
        st_f[...] = jnp.zeros_like(st_f)
        st_b[...] = jnp.zeros_like(st_b)

    logits = lbl_ref[...]
    mx = jnp.max(logits, axis=0, keepdims=True)
    ex = jnp.exp(logits - mx)
    prob = ex / jnp.sum(ex, axis=0, keepdims=True)
    lb_ref[...] = jnp.sum(prob[0:layer + 1], axis=0, keepdims=True) - prob[0:1]

    consts_f, consts_b = _hgrn_consts(False), _hgrn_consts(True)

    def head(h, carry):
        cs = pl.ds(pl.multiple_of(h * LANES, LANES), LANES)
        dirs = ((qf_ref, ff_ref, vf_ref, of_ref, st_f, consts_f, False),
                (qb_ref, fb_ref, vb_ref, ob_ref, st_b, consts_b, True))
        gens = [_hgrn_head(q_ref[0, :, cs].astype(F32), f_ref[0, :, cs].astype(F32), v_ref[0, :, cs].astype(F32),
                           lb_ref[:, cs], st_ref[h], consts, rev)
                for q_ref, f_ref, v_ref, _, st_ref, consts, rev in dirs]
        results = [None, None]
        while results[0] is None:
            results = [next(g) for g in gens]
        for (_, _, _, o_ref, st_ref, _, _), (o, st) in zip(dirs, results):
            o_ref[0, :, cs] = o.astype(o_ref.dtype)
            st_ref[h] = st
        return carry

    lax.fori_loop(0, HG_HEADS, head, 0, unroll=True)


def _hgrn_scan(h, lb_logits, layer):
    bsz, t, _ = h.shape
    assert t % HG_BLOCK == 0
    nb = t // HG_BLOCK
    blk = (1, HG_BLOCK, HG_F)
    fwd = lambda col: pl.BlockSpec(blk, lambda b, j: (b, j, col))
    bwd = lambda col: pl.BlockSpec(blk, lambda b, j: (b, nb - 1 - j, col))
    out_sd = jax.ShapeDtypeStruct((bsz, t, D_PRIM), BF16)
    state = pltpu.VMEM((HG_HEADS, HG_DV, HG_DK), F32)
    return pl.pallas_call(
        functools.partial(_hgrn_kernel, layer=layer),
        grid=(bsz, nb),
        in_specs=[_resident(lb_logits.shape), fwd(0), fwd(1), fwd(3), bwd(0), bwd(2), bwd(3)],
        out_specs=[fwd(0), bwd(0)],
        out_shape=[out_sd, out_sd],
        scratch_shapes=[state, state, pltpu.VMEM((1, HG_F), F32)],
        compiler_params=_params("parallel", "arbitrary"),
        name="hgrn_scan",
    )(lb_logits, h, h, h, h, h, h)


def _layer_norm(y, g, b):
    mu = jnp.mean(y, axis=-1, keepdims=True)
    yc = y - mu
    var = jnp.mean(yc * yc, axis=-1, keepdims=True)
    return yc * lax.rsqrt(var + LN_EPS) * g + b


def _mem_attention(qm_ref, mkv_ref, mem_out):
    lo = lax.broadcasted_iota(jnp.int32, (1, LANES), 1) < CROSS_DH
    units = [(slice(t * ATT_SUB, (t + 1) * ATT_SUB), p)
             for t in range(qm_ref.shape[1] // ATT_SUB) for p in range(CROSS_HEADS // 2)]
    scores = []
    for rws, p in units:
        cols = slice(p * LANES, (p + 1) * LANES)
        q = qm_ref[0, rws, cols] * (CROSS_DH ** -0.5)
        zero = jnp.zeros_like(q)
        q2 = jnp.concatenate([jnp.where(lo, q, zero), jnp.where(lo, zero, q)], axis=0)
        scores.append(lax.dot_general(q2, mkv_ref[0, :, cols], (((1,), (1,)), ((), ())),
                                      preferred_element_type=F32))
    yield None
    for (rws, p), s in zip(units, scores):
        cols = slice(p * LANES, (p + 1) * LANES)
        mv = mkv_ref[0, :, D_CROSS + p * LANES:D_CROSS + (p + 1) * LANES]
        mx = jnp.max(s, axis=-1, keepdims=True)
        e = jnp.exp(s - mx)
        den = jnp.sum(e, axis=-1, keepdims=True)
        o = jnp.dot(e.astype(BF16), mv, preferred_element_type=F32) / den
        mem_out[rws, cols] = jnp.where(lo, o[0:ATT_SUB], o[ATT_SUB:2 * ATT_SUB]).astype(BF16)
    yield None


def _mix_ffn_tail(x_ref, prim, mem_attn, mem_out, wo_ref, g1_ref, b1_ref, w1_ref, w2_ref, g2_ref, b2_ref, o_ref):
    next(mem_attn)
    mix = jnp.dot(prim, wo_ref[0:D_PRIM, :], preferred_element_type=F32)
    next(mem_attn)
    mix = mix + jnp.dot(mem_out[...], wo_ref[D_PRIM:D_MODEL, :], preferred_element_type=F32)
    x1 = _layer_norm(ALPHA * x_ref[0] + mix, g1_ref[...], b1_ref[...])
    x1b = x1.astype(BF16)
    acc = jnp.zeros(x1.shape, F32)
    for c in range(0, D_FF, FF_CHUNK):
        hid = jnp.dot(x1b, w1_ref[:, c:c + FF_CHUNK], preferred_element_type=F32)
        hid = jnp.square(jnp.maximum(hid, 0.0)).astype(BF16)
        acc = acc + jnp.dot(hid, w2_ref[c:c + FF_CHUNK, :], preferred_element_type=F32)
    o_ref[0] = _layer_norm(ALPHA * x1 + acc, g2_ref[...], b2_ref[...])


def _mix_ffn_a_kernel(x_ref, prim_ref, qm_ref, mkv_ref, wo_ref, g1_ref, b1_ref, w1_ref, w2_ref, g2_ref, b2_ref,
                      o_ref, mem_out):
    _mix_ffn_tail(x_ref, prim_ref[0], _mem_attention(qm_ref, mkv_ref, mem_out), mem_out, wo_ref, g1_ref, b1_ref,
                  w1_ref, w2_ref, g2_ref, b2_ref, o_ref)


def _mix_ffn_b_kernel(x_ref, of_ref, ob_ref, gate_ref, nw_ref, qm_ref, mkv_ref, wo_ref, g1_ref, b1_ref, w1_ref,
                      w2_ref, g2_ref, b2_ref, o_ref, mem_out):
    o = of_ref[0].astype(F32) + ob_ref[0].astype(F32)
    gate = gate_ref[0].astype(F32)
    gate = gate * jax.nn.sigmoid(gate)
    nw = nw_ref[...]
    heads = []
    for h in range(HG_HEADS):
        cs = slice(h * HG_DV, (h + 1) * HG_DV)
        oh = o[:, cs]
        oh = oh * lax.rsqrt(jnp.mean(oh * oh, axis=-1, keepdims=True) + RMS_EPS) * nw
        heads.append((oh * gate[:, cs]).astype(BF16))
    prim = jnp.concatenate(heads, axis=1)
    _mix_ffn_tail(x_ref, prim, _mem_attention(qm_ref, mkv_ref, mem_out), mem_out, wo_ref, g1_ref, b1_ref, w1_ref,
                  w2_ref, g2_ref, b2_ref, o_ref)


def _mix_ffn(x, mixer_in, h, qm_col, mem_kv, w_out, ln1_g, ln1_b, w_ff1, w_ff2, ln2_g, ln2_b, norm_w=None):
    bsz, t, d = x.shape
    tm = ROW_TILE
    assert t % tm == 0
    tile = lambda w, col: pl.BlockSpec((1, tm, w), lambda b, i: (b, i, col))
    vec = lambda a: a.reshape(1, -1)
    common_specs = [tile(D_CROSS, qm_col), pl.BlockSpec((1, N_MEM, 2 * D_CROSS), lambda b, i: (b, 0, 0)),
                    _resident(w_out.shape), _resident((1, d)), _resident((1, d)), _resident(w_ff1.shape),
                    _resident(w_ff2.shape), _resident((1, d)), _resident((1, d))]
    common_args = [h, mem_kv, w_out, vec(ln1_g), vec(ln1_b), w_ff1, w_ff2, vec(ln2_g), vec(ln2_b)]
    if norm_w is None:
        body, name = _mix_ffn_a_kernel, "mix_ffn_a"
        specs = [tile(d, 0), tile(D_PRIM, 0)] + common_specs
        args = [x, mixer_in[0]] + common_args
    else:
        body, name = _mix_ffn_b_kernel, "mix_ffn_b"
        specs = [tile(d, 0), tile(D_PRIM, 0), tile(D_PRIM, 0), tile(D_PRIM, 4), _resident((1, HG_DV))] + common_specs
        args = [x, mixer_in[0], mixer_in[1], h, vec(norm_w)] + common_args
    return pl.pallas_call(
        body,
        grid=(bsz, t // tm),
        in_specs=specs,
        out_specs=tile(d, 0),
        out_shape=jax.ShapeDtypeStruct((bsz, t, d), F32),
        scratch_shapes=[pltpu.VMEM((tm, D_CROSS), BF16)],
        compiler_params=_params("parallel", "parallel"),
        name=name,
    )(*args)


def _trunk(x, mem, w_mem_kv, w_in_a, na_bias, w_in_b, lb_logits, hg_norm_w, w_out, ln1_g, ln1_b, w_ff1, w_ff2,
           ln2_g, ln2_b):
    bsz, t, d = x.shape
    rows = t // GRID_W
    mem_kv = _proj(mem.reshape(bsz * N_MEM, d), w_mem_kv, ROW_TILE).reshape(bsz, N_MEM, 2 * D_CROSS)
    for layer in range(DEPTH):
        j = layer // 2
        if layer % 2 == 0:
            h = _proj(x.reshape(bsz * t, d), w_in_a[j], ROW_TILE).reshape(bsz, t, -1)
            mixer_in = (_na_attn(h, na_bias[j], rows),)
            qm_col, norm_w = 3 * D_PRIM // D_CROSS, None
        else:
            h = _proj(x.reshape(bsz * t, d), w_in_b[j], ROW_TILE).reshape(bsz, t, -1)
            mixer_in = _hgrn_scan(h, lb_logits, layer)
            qm_col, norm_w = (3 * HG_F + 2 * D_PRIM) // D_CROSS, hg_norm_w[j]
        x = _mix_ffn(x, mixer_in, h, qm_col, mem_kv, w_out[layer], ln1_g[layer], ln1_b[layer], w_ff1[layer],
                     w_ff2[layer], ln2_g[layer], ln2_b[layer], norm_w)
    return x


def kernel(x_prompt, x_sample, mem_prompt, mem_sample, w_mem_kv, w_in_a, rpb, w_in_b, lb_logits, hg_norm_w, w_out,
           ln1_g, ln1_b, w_ff1, w_ff2, ln2_g, ln2_b):
    bf = lambda a: a.astype(BF16)
    na_bias = jax.vmap(_na_bias_table)(rpb)
    shared = (bf(w_mem_kv), bf(w_in_a), na_bias, bf(w_in_b), lb_logits.astype(F32), hg_norm_w, bf(w_out), ln1_g,
              ln1_b, bf(w_ff1), bf(w_ff2), ln2_g, ln2_b)
    return (_trunk(x_prompt, mem_prompt, *shared), _trunk(x_sample, mem_sample, *shared))
```

```python
import functools

import jax
import jax.numpy as jnp
from jax import lax
from jax.experimental import pallas as pl
from jax.experimental.pallas import tpu as pltpu

F32 = jnp.float32
BF16 = jnp.bfloat16

D_MODEL = 1024
DEPTH = 2
GRID_W = 64
WIN_R = 8
WIN_C = 16
N_MEM = 256
CROSS_HEADS = 4
CROSS_DH = 64
D_CROSS = CROSS_HEADS * CROSS_DH
D_PRIM = D_MODEL - D_CROSS
NA_HEADS = 12
NA_DH = D_PRIM // NA_HEADS
HG_HEADS = 6
HG_DK = 128
HG_DV = D_PRIM // HG_HEADS
HG_F = HG_HEADS * HG_DK
D_FF = 4 * D_MODEL
ALPHA = (2 * DEPTH) ** 0.25
LN_EPS = 1e-5
RMS_EPS = 1e-6

LANES = 128
SUBLANES = 8
VMEM_LIMIT = 56 * 1024 * 1024

ROW_TILE = 512
NA_ROWS = 8
NA_TOK = NA_ROWS * GRID_W
NA_PAIRS = NA_HEADS // 2
HG_CHUNK = 32
HG_SUB = SUBLANES
HG_NSUB = HG_CHUNK // HG_SUB
HG_BLOCK = 256
HG_NCHUNK = HG_BLOCK // HG_CHUNK
FF_CHUNK = 1024
ATT_SUB = 128


def _params(*sem):
    return pltpu.CompilerParams(dimension_semantics=sem, vmem_limit_bytes=VMEM_LIMIT)


def _resident(shape):
    zeros = (0,) * len(shape)
    return pl.BlockSpec(shape, lambda *_: zeros, pipeline_mode=pl.Buffered(1))


def _proj_kernel(x_ref, w_ref, o_ref, *, n_chunk):
    xb = x_ref[...].astype(BF16)
    for c in range(0, o_ref.shape[1], n_chunk):
        acc = jnp.dot(xb, w_ref[:, c:c + n_chunk], preferred_element_type=F32)
        o_ref[:, c:c + n_chunk] = acc.astype(o_ref.dtype)


def _proj(x2d, w, tm):
    n_rows, d = x2d.shape
    n_out = w.shape[1]
    tm = min(tm, n_rows)
    assert n_rows % tm == 0
    n_chunk = 512 if n_out % 512 == 0 else n_out
    return pl.pallas_call(
        functools.partial(_proj_kernel, n_chunk=n_chunk),
        grid=(n_rows // tm,),
        in_specs=[pl.BlockSpec((tm, d), lambda i: (i, 0)), _resident((d, n_out))],
        out_specs=pl.BlockSpec((tm, n_out), lambda i: (i, 0)),
        out_shape=jax.ShapeDtypeStruct((n_rows, n_out), BF16),
        compiler_params=_params("parallel"),
        name="proj",
    )(x2d, w)


def _na_bias_table(rpb):
    qc = jnp.arange(GRID_W)[:, None]
    kc = jnp.arange(GRID_W)[None, :]
    cstart = jnp.clip(qc - WIN_C // 2, 0, GRID_W - WIN_C)
    valid = (kc >= cstart) & (kc < cstart + WIN_C)
    dcol = jnp.clip(kc - qc + WIN_C - 1, 0, 2 * WIN_C - 2)
    t = jnp.where(valid, rpb[:, :, dcol], -jnp.inf)
    nd = 2 * WIN_R - 2
    x = jnp.stack([t[:, :nd], t[:, 1:]], axis=3)
    x = x.reshape(NA_PAIRS, 2, nd, GRID_W, 2 * GRID_W)
    return jnp.transpose(x, (0, 2, 1, 3, 4)).reshape(NA_PAIRS, nd, 2 * GRID_W, 2 * GRID_W).astype(F32)


def _na_kernel(q_ref, kp_ref, kc_ref, kn_ref, vp_ref, vc_ref, vn_ref, bias_ref, o_ref, kcat, vcat, s_scr, e_scr,
               m_scr, d_scr, *, rows):
    j = pl.program_id(1)
    kcat[0:NA_TOK] = kp_ref[0]
    kcat[NA_TOK:2 * NA_TOK] = kc_ref[0]
    kcat[2 * NA_TOK:3 * NA_TOK] = kn_ref[0]
    vcat[0:NA_TOK] = vp_ref[0]
    vcat[NA_TOK:2 * NA_TOK] = vc_ref[0]
    vcat[2 * NA_TOK:3 * NA_TOK] = vn_ref[0]

    lo = lax.broadcasted_iota(jnp.int32, (1, LANES), 1) < NA_DH
    n_keys = WIN_R * GRID_W

    def window(i):
        r = j * NA_ROWS + i
        rs = jnp.clip(r - WIN_R // 2, 0, rows - WIN_R)
        start = pl.multiple_of((rs - j * NA_ROWS + NA_ROWS) * GRID_W, GRID_W)
        return r - rs, pl.ds(start, n_keys), pl.ds(pl.multiple_of(i * GRID_W, GRID_W), GRID_W)

    rep = (2 * GRID_W, LANES)

    def score_body(i, carry):
        delta, keys, qrow = window(i)
        for p in range(NA_PAIRS):
            cols = slice(p * LANES, (p + 1) * LANES)
            q = q_ref[0, qrow, cols] * (NA_DH ** -0.5)
            zero = jnp.zeros_like(q)
            q2 = jnp.concatenate([jnp.where(lo, q, zero), jnp.where(lo, zero, q)], axis=0)
            s = lax.dot_general(q2, kcat[keys, cols], (((1,), (1,)), ((), ())), preferred_element_type=F32)
            bias = jnp.concatenate(
                [bias_ref[p, WIN_R - 1 - delta + 2 * m] for m in range(WIN_R // 2)], axis=1)
            s = s + bias
            s_scr[i * NA_PAIRS + p] = s
            m_scr[i * NA_PAIRS + p] = jnp.broadcast_to(jnp.max(s, axis=-1, keepdims=True), rep)
        return carry

    lax.fori_loop(0, NA_ROWS, score_body, 0, unroll=True)

    def exp_body(u, carry):
        e = jnp.exp(s_scr[u] - jnp.tile(m_scr[u], (1, n_keys // LANES)))
        d_scr[u] = jnp.broadcast_to(jnp.sum(e, axis=-1, keepdims=True), rep)
        e_scr[u] = e.astype(BF16)
        return carry

    lax.fori_loop(0, NA_ROWS * NA_PAIRS, exp_body, 0, unroll=24)

    def value_body(i, carry):
        _, keys, qrow = window(i)
        for p in range(NA_PAIRS):
            cols = slice(p * LANES, (p + 1) * LANES)
            u = i * NA_PAIRS + p
            o = jnp.dot(e_scr[u], vcat[keys, cols], preferred_element_type=F32) / d_scr[u]
            out = jnp.where(lo, o[0:GRID_W], o[GRID_W:2 * GRID_W])
            o_ref[0, qrow, cols] = out.astype(o_ref.dtype)
        return carry

    lax.fori_loop(0, NA_ROWS, value_body, 0, unroll=True)


def _na_attn(h, bias, rows):
    bsz, t, _ = h.shape
    assert rows % NA_ROWS == 0 and rows >= WIN_R and t == rows * GRID_W
    nb = rows // NA_ROWS
    blk = (1, NA_TOK, D_PRIM)
    n_units = NA_ROWS * NA_PAIRS

    def spec(col, shift):
        return pl.BlockSpec(blk, lambda b, j: (b, jnp.clip(j + shift, 0, nb - 1), col))

    return pl.pallas_call(
        functools.partial(_na_kernel, rows=rows),
        grid=(bsz, nb),
        in_specs=[spec(0, 0), spec(1, -1), spec(1, 0), spec(1, 1), spec(2, -1), spec(2, 0), spec(2, 1),
                  _resident(bias.shape)],
        out_specs=pl.BlockSpec(blk, lambda b, j: (b, j, 0)),
        out_shape=jax.ShapeDtypeStruct((bsz, t, D_PRIM), BF16),
        scratch_shapes=[pltpu.VMEM((3 * NA_TOK, D_PRIM), BF16), pltpu.VMEM((3 * NA_TOK, D_PRIM), BF16),
                        pltpu.VMEM((n_units, 2 * GRID_W, WIN_R * GRID_W), F32),
                        pltpu.VMEM((n_units, 2 * GRID_W, WIN_R * GRID_W), BF16),
                        pltpu.VMEM((n_units, 2 * GRID_W, LANES), F32), pltpu.VMEM((n_units, 2 * GRID_W, LANES), F32)],
        compiler_params=_params("parallel", "parallel"),
        name="na_attn",
    )(h, h, h, h, h, h, h, bias)


def _hgrn_consts(rev):
    t = lax.broadcasted_iota(jnp.int32, (HG_BLOCK, HG_BLOCK), 0)
    s = lax.broadcasted_iota(jnp.int32, (HG_BLOCK, HG_BLOCK), 1)
    seen = (s >= t) if rev else (s <= t)
    tri = jnp.where(seen & (s // HG_CHUNK == t // HG_CHUNK), 1.0, 0.0).astype(BF16)
    kr = lax.broadcasted_iota(jnp.int32, (2 * LANES, LANES), 0)
    kl = lax.broadcasted_iota(jnp.int32, (2 * LANES, LANES), 1)
    fold = jnp.where(kr // LANES == kl // HG_CHUNK, 1.0, 0.0).astype(BF16)
    trow = lax.broadcasted_iota(jnp.int32, (HG_CHUNK, LANES), 0)
    lane = lax.broadcasted_iota(jnp.int32, (HG_CHUNK, LANES), 1)
    key = lane & (HG_CHUNK - 1)
    same = (key // HG_SUB == trow // HG_SUB) & (lane < 2 * HG_CHUNK)
    diag_hit = []
    for d in range(0, HG_SUB, 2):
        dist = jnp.where(lane < HG_CHUNK, d, d + 1)
        diag_hit.append(same & (key == (trow + dist if rev else trow - dist)))
    sub_lane = lax.broadcasted_iota(jnp.int32, (HG_SUB, LANES), 1)
    off_cols = [(sub_lane >= i * HG_SUB) & (sub_lane < (i + 1) * HG_SUB) for i in range(HG_NSUB)]
    return tri, fold, diag_hit, off_cols


def _hgrn_head(ql, fl, v, lb, st, consts, rev):
    tri, fold, diag_hit, off_cols = consts
    qs = ql * jax.nn.sigmoid(ql)
    f = lb + (1.0 - lb) * jax.nn.sigmoid(fl)
    k = 1.0 - f
    g = jnp.log(f)
    g_hi = g.astype(BF16)
    g_lo = (g - g_hi.astype(F32)).astype(BF16)
    b = jnp.dot(tri, jnp.concatenate([g_hi, g_lo], axis=1), preferred_element_type=F32)
    b = b[:, :LANES] + b[:, LANES:]
    yield None

    def pos(p):
        return HG_NSUB - 1 - p if rev else p

    def sub(a, c, p):
        r0 = c * HG_CHUNK + pos(p) * HG_SUB
        return a[r0:r0 + HG_SUB]

    def end(c, p):
        r = c * HG_CHUNK + pos(p) * HG_SUB + (0 if rev else HG_SUB - 1)
        return b[r:r + 1]

    pairs = [(i, j) for i in range(1, HG_NSUB) for j in range(i)]
    q_bar, k_hat, k_bar, lhs_off, decay = {}, {}, {}, [], []
    for c in range(HG_NCHUNK):
        ends = [end(c, p) for p in range(HG_NSUB)]
        q_hat = []
        for p in range(HG_NSUB):
            bb = sub(b, c, p)
            q_hat.append(sub(qs, c, p) * jnp.exp(bb if p == 0 else bb - ends[p - 1]))
            kh = sub(k, c, p) * jnp.exp(ends[p] - bb)
            k_hat[c, pos(p)] = kh
            q_bar[c, pos(p)] = q_hat[p] if p == 0 else q_hat[p] * jnp.exp(ends[p - 1])
            k_bar[c, pos(p)] = kh if p == HG_NSUB - 1 else kh * jnp.exp(ends[-1] - ends[p])
        lhs_off += [q_hat[i] if i == j + 1 else q_hat[i] * jnp.exp(ends[i - 1] - ends[j]) for i, j in pairs]
        decay.append(jnp.exp(ends[-1]))

    def rows_cat(parts):
        return jnp.concatenate([parts[c, i] for c in range(HG_NCHUNK) for i in range(HG_NSUB)], axis=0).astype(BF16)

    q_bar, k_hat, k_bar = rows_cat(q_bar), rows_cat(k_hat), rows_cat(k_bar)
    lhs_off = jnp.concatenate(lhs_off, axis=0).astype(BF16)
    vb = v.astype(BF16)

    shift = HG_SUB - 1 if rev else 1

    def prev(a):
        a3 = a.reshape(HG_BLOCK // HG_SUB, HG_SUB, LANES)
        return pltpu.roll(a3, shift, 1).reshape(HG_BLOCK, LANES)

    kd = k
    terms = [qs * k]
    for d in range(1, HG_SUB):
        kd = f * prev(kd)
        terms.append(qs * kd)
    terms = [x.astype(BF16) for x in terms]
    z = jnp.concatenate([jnp.concatenate([terms[d], terms[d + 1]], axis=1) for d in range(0, HG_SUB, 2)], axis=0)

    zpad = jnp.zeros((LANES - HG_CHUNK, LANES), BF16)
    vpad = jnp.zeros((LANES - 2 * HG_CHUNK, LANES), BF16)
    n_off = len(pairs) * HG_SUB
    chunk_rows = [slice(c * HG_CHUNK, (c + 1) * HG_CHUNK) for c in range(HG_NCHUNK)]
    r_offs = [lax.dot_general(lhs_off[c * n_off:(c + 1) * n_off], jnp.concatenate([k_hat[rs], zpad], axis=0),
                              (((1,), (1,)), ((), ())), preferred_element_type=F32)
              for c, rs in enumerate(chunk_rows)]
    rd = jnp.dot(z, fold, preferred_element_type=F32)
    upd = [lax.dot_general(vb[rs], k_bar[rs], (((0,), (0,)), ((), ())), preferred_element_type=F32)
           for rs in chunk_rows]
    yield None

    o_intra = []
    for c, rs in enumerate(chunk_rows):
        r_off = r_offs[c]
        a_rows = []
        for i in range(HG_NSUB):
            p = pos(i)
            acc = jnp.zeros((HG_SUB, LANES), F32)
            for j in range(p):
                n = pairs.index((p, j))
                acc = jnp.where(off_cols[pos(j)], r_off[n * HG_SUB:(n + 1) * HG_SUB], acc)
            a_rows.append(acc)
        a = jnp.concatenate(a_rows, axis=0)
        for n in range(HG_SUB // 2):
            a = a + jnp.where(diag_hit[n], rd[n * HG_BLOCK + c * HG_CHUNK:n * HG_BLOCK + (c + 1) * HG_CHUNK], 0.0)
        v_c = vb[rs]
        v2 = jnp.concatenate([v_c, v_c, vpad], axis=0)
        o_intra.append(jnp.dot(a.astype(BF16), v2, preferred_element_type=F32))
    yield None

    outs = [None] * HG_NCHUNK
    for step in range(HG_NCHUNK):
        c = HG_NCHUNK - 1 - step if rev else step
        outs[c] = o_intra[c] + lax.dot_general(q_bar[chunk_rows[c]], st.astype(BF16), (((1,), (1,)), ((), ())),
                                               preferred_element_type=F32)
        st = st * decay[c] + upd[c]
    yield jnp.concatenate(outs, axis=0), st


def _hgrn_kernel(lbl_ref, qf_ref, ff_ref, vf_ref, qb_ref, fb_ref, vb_ref, of_ref, ob_ref, st_f, st_b, lb_ref, *,
                 layer):
    @pl.when(pl.program_id(1) == 0)
    def _():
        st_f[...] = jnp.zeros_like(st_f)
        st_b[...] = jnp.zeros_like(st_b)

    logits = lbl_ref[...]
    mx = jnp.max(logits, axis=0, keepdims=True)
    ex = jnp.exp(logits - mx)
    prob = ex / jnp.sum(ex, axis=0, keepdims=True)
    lb_ref[...] = jnp.sum(prob[0:layer + 1], axis=0, keepdims=True) - prob[0:1]

    consts_f, consts_b = _hgrn_consts(False), _hgrn_consts(True)

    def head(h, carry):
        cs = pl.ds(pl.multiple_of(h * LANES, LANES), LANES)
        dirs = ((qf_ref, ff_ref, vf_ref, of_ref, st_f, consts_f, False),
                (qb_ref, fb_ref, vb_ref, ob_ref, st_b, consts_b, True))
        gens = [_hgrn_head(q_ref[0, :, cs].astype(F32), f_ref[0, :, cs].astype(F32), v_ref[0, :, cs].astype(F32),
                           lb_ref[:, cs], st_ref[h], consts, rev)
                for q_ref, f_ref, v_ref, _, st_ref, consts, rev in dirs]
        results = [None, None]
        while results[0] is None:
            results = [next(g) for g in gens]
        for (_, _, _, o_ref, st_ref, _, _), (o, st) in zip(dirs, results):
            o_ref[0, :, cs] = o.astype(o_ref.dtype)
            st_ref[h] = st
        return carry

    lax.fori_loop(0, HG_HEADS, head, 0, unroll=True)


def _hgrn_scan(h, lb_logits, layer):
    bsz, t, _ = h.shape
    assert t % HG_BLOCK == 0
    nb = t // HG_BLOCK
    blk = (1, HG_BLOCK, HG_F)
    fwd = lambda col: pl.BlockSpec(blk, lambda b, j: (b, j, col))
    bwd = lambda col: pl.BlockSpec(blk, lambda b, j: (b, nb - 1 - j, col))
    out_sd = jax.ShapeDtypeStruct((bsz, t, D_PRIM), BF16)
    state = pltpu.VMEM((HG_HEADS, HG_DV, HG_DK), F32)
    return pl.pallas_call(
        functools.partial(_hgrn_kernel, layer=layer),
        grid=(bsz, nb),
        in_specs=[_resident(lb_logits.shape), fwd(0), fwd(1), fwd(3), bwd(0), bwd(2), bwd(3)],
        out_specs=[fwd(0), bwd(0)],
        out_shape=[out_sd, out_sd],
        scratch_shapes=[state, state, pltpu.VMEM((1, HG_F), F32)],
        compiler_params=_params("parallel", "arbitrary"),
        name="hgrn_scan",
    )(lb_logits, h, h, h, h, h, h)


def _layer_norm(y, g, b):
    mu = jnp.mean(y, axis=-1, keepdims=True)
    yc = y - mu
    var = jnp.mean(yc * yc, axis=-1, keepdims=True)
    return yc * lax.rsqrt(var + LN_EPS) * g + b


def _mem_attention(qm_ref, mkv_ref, mem_out):
    lo = lax.broadcasted_iota(jnp.int32, (1, LANES), 1) < CROSS_DH
    units = [(slice(t * ATT_SUB, (t + 1) * ATT_SUB), p)
             for t in range(qm_ref.shape[1] // ATT_SUB) for p in range(CROSS_HEADS // 2)]
    scores = []
    for rws, p in units:
        cols = slice(p * LANES, (p + 1) * LANES)
        q = qm_ref[0, rws, cols] * (CROSS_DH ** -0.5)
        zero = jnp.zeros_like(q)
        q2 = jnp.concatenate([jnp.where(lo, q, zero), jnp.where(lo, zero, q)], axis=0)
        scores.append(lax.dot_general(q2, mkv_ref[0, :, cols], (((1,), (1,)), ((), ())),
                                      preferred_element_type=F32))
    yield None
    for (rws, p), s in zip(units, scores):
        cols = slice(p * LANES, (p + 1) * LANES)
        mv = mkv_ref[0, :, D_CROSS + p * LANES:D_CROSS + (p + 1) * LANES]
        mx = jnp.max(s, axis=-1, keepdims=True)
        e = jnp.exp(s - mx)
        den = jnp.sum(e, axis=-1, keepdims=True)
        o = jnp.dot(e.astype(BF16), mv, preferred_element_type=F32) / den
        mem_out[rws, cols] = jnp.where(lo, o[0:ATT_SUB], o[ATT_SUB:2 * ATT_SUB]).astype(BF16)
    yield None


def _mix_ffn_tail(x_ref, prim, mem_attn, mem_out, wo_ref, g1_ref, b1_ref, w1_ref, w2_ref, g2_ref, b2_ref, o_ref):
    next(mem_attn)
    mix = jnp.dot(prim, wo_ref[0:D_PRIM, :], preferred_element_type=F32)
    next(mem_attn)
    mix = mix + jnp.dot(mem_out[...], wo_ref[D_PRIM:D_MODEL, :], preferred_element_type=F32)
    half = mix.shape[0] // 2
    x1s = []
    for r in range(2):
        rs = slice(r * half, (r + 1) * half)
        x1s.append(_layer_norm(ALPHA * x_ref[0, rs, :] + mix[rs], g1_ref[...], b1_ref[...]))
    for r in range(2):
        rs = slice(r * half, (r + 1) * half)
        x1 = x1s[r]
        x1b = x1.astype(BF16)
        acc = jnp.zeros(x1.shape, F32)
        for c in range(0, D_FF, FF_CHUNK):
            hid = jnp.dot(x1b, w1_ref[:, c:c + FF_CHUNK], preferred_element_type=F32)
            hid = jnp.square(jnp.maximum(hid, 0.0)).astype(BF16)
            acc = acc + jnp.dot(hid, w2_ref[c:c + FF_CHUNK, :], preferred_element_type=F32)
        o_ref[0, rs, :] = _layer_norm(ALPHA * x1 + acc, g2_ref[...], b2_ref[...])


def _mix_ffn_a_kernel(x_ref, prim_ref, qm_ref, mkv_ref, wo_ref, g1_ref, b1_ref, w1_ref, w2_ref, g2_ref, b2_ref,
                      o_ref, mem_out):
    _mix_ffn_tail(x_ref, prim_ref[0], _mem_attention(qm_ref, mkv_ref, mem_out), mem_out, wo_ref, g1_ref, b1_ref,
                  w1_ref, w2_ref, g2_ref, b2_ref, o_ref)


def _mix_ffn_b_kernel(x_ref, of_ref, ob_ref, gate_ref, nw_ref, qm_ref, mkv_ref, wo_ref, g1_ref, b1_ref, w1_ref,
                      w2_ref, g2_ref, b2_ref, o_ref, mem_out):
    o = of_ref[0].astype(F32) + ob_ref[0].astype(F32)
    gate = gate_ref[0].astype(F32)
    gate = gate * jax.nn.sigmoid(gate)
    nw = nw_ref[...]
    heads = []
    for h in range(HG_HEADS):
        cs = slice(h * HG_DV, (h + 1) * HG_DV)
        oh = o[:, cs]
        oh = oh * lax.rsqrt(jnp.mean(oh * oh, axis=-1, keepdims=True) + RMS_EPS) * nw
        heads.append((oh * gate[:, cs]).astype(BF16))
    prim = jnp.concatenate(heads, axis=1)
    _mix_ffn_tail(x_ref, prim, _mem_attention(qm_ref, mkv_ref, mem_out), mem_out, wo_ref, g1_ref, b1_ref, w1_ref,
                  w2_ref, g2_ref, b2_ref, o_ref)


def _mix_ffn(x, mixer_in, h, qm_col, mem_kv, w_out, ln1_g, ln1_b, w_ff1, w_ff2, ln2_g, ln2_b, norm_w=None):
    bsz, t, d = x.shape
    tm = ROW_TILE
    assert t % tm == 0
    tile = lambda w, col: pl.BlockSpec((1, tm, w), lambda b, i: (b, i, col))
    vec = lambda a: a.reshape(1, -1)
    common_specs = [tile(D_CROSS, qm_col), pl.BlockSpec((1, N_MEM, 2 * D_CROSS), lambda b, i: (b, 0, 0)),
                    _resident(w_out.shape), _resident((1, d)), _resident((1, d)), _resident(w_ff1.shape),
                    _resident(w_ff2.shape), _resident((1, d)), _resident((1, d))]
    common_args = [h, mem_kv, w_out, vec(ln1_g), vec(ln1_b), w_ff1, w_ff2, vec(ln2_g), vec(ln2_b)]
    if norm_w is None:
        body, name = _mix_ffn_a_kernel, "mix_ffn_a"
        specs = [tile(d, 0), tile(D_PRIM, 0)] + common_specs
        args = [x, mixer_in[0]] + common_args
    else:
        body, name = _mix_ffn_b_kernel, "mix_ffn_b"
        specs = [tile(d, 0), tile(D_PRIM, 0), tile(D_PRIM, 0), tile(D_PRIM, 4), _resident((1, HG_DV))] + common_specs
        args = [x, mixer_in[0], mixer_in[1], h, vec(norm_w)] + common_args
    return pl.pallas_call(
        body,
        grid=(bsz, t // tm),
        in_specs=specs,
        out_specs=tile(d, 0),
        out_shape=jax.ShapeDtypeStruct((bsz, t, d), F32),
        scratch_shapes=[pltpu.VMEM((tm, D_CROSS), BF16)],
        compiler_params=_params("parallel", "parallel"),
        name=name,
    )(*args)


def _trunk(x, mem, w_mem_kv, w_in_a, na_bias, w_in_b, lb_logits, hg_norm_w, w_out, ln1_g, ln1_b, w_ff1, w_ff2,
           ln2_g, ln2_b):
    bsz, t, d = x.shape
    rows = t // GRID_W
    mem_kv = _proj(mem.reshape(bsz * N_MEM, d), w_mem_kv, ROW_TILE).reshape(bsz, N_MEM, 2 * D_CROSS)
    for layer in range(DEPTH):
        j = layer // 2
        if layer % 2 == 0:
            h = _proj(x.reshape(bsz * t, d), w_in_a[j], ROW_TILE).reshape(bsz, t, -1)
            mixer_in = (_na_attn(h, na_bias[j], rows),)
            qm_col, norm_w = 3 * D_PRIM // D_CROSS, None
        else:
            h = _proj(x.reshape(bsz * t, d), w_in_b[j], ROW_TILE).reshape(bsz, t, -1)
            mixer_in = _hgrn_scan(h, lb_logits, layer)
            qm_col, norm_w = (3 * HG_F + 2 * D_PRIM) // D_CROSS, hg_norm_w[j]
        x = _mix_ffn(x, mixer_in, h, qm_col, mem_kv, w_out[layer], ln1_g[layer], ln1_b[layer], w_ff1[layer],
                     w_ff2[layer], ln2_g[layer], ln2_b[layer], norm_w)
    return x


def kernel(x_prompt, x_sample, mem_prompt, mem_sample, w_mem_kv, w_in_a, rpb, w_in_b, lb_logits, hg_norm_w, w_out,
           ln1_g, ln1_b, w_ff1, w_ff2, ln2_g, ln2_b):
    bf = lambda a: a.astype(BF16)
    na_bias = jax.vmap(_na_bias_table)(rpb)
    shared = (bf(w_mem_kv), bf(w_in_a), na_bias, bf(w_in_b), lb_logits.astype(F32), hg_norm_w, bf(w_out), ln1_g,
              ln1_b, bf(w_ff1), bf(w_ff2), ln2_g, ln2_b)
    return (_trunk(x_prompt, mem_prompt, *shared), _trunk(x_sample, mem_sample, *shared))
```

```python
import functools

import jax
import jax.numpy as jnp
from jax import lax
from jax.experimental import pallas as pl
from jax.experimental.pallas import tpu as pltpu

F32 = jnp.float32
BF16 = jnp.bfloat16

D_MODEL = 1024
DEPTH = 2
GRID_W = 64
WIN_R = 8
WIN_C = 16
N_MEM = 256
CROSS_HEADS = 4
CROSS_DH = 64
D_CROSS = CROSS_HEADS * CROSS_DH
D_PRIM = D_MODEL - D_CROSS
NA_HEADS = 12
NA_DH = D_PRIM // NA_HEADS
HG_HEADS = 6
HG_DK = 128
HG_DV = D_PRIM // HG_HEADS
HG_F = HG_HEADS * HG_DK
D_FF = 4 * D_MODEL
ALPHA = (2 * DEPTH) ** 0.25
LN_EPS = 1e-5
RMS_EPS = 1e-6

LANES = 128
SUBLANES = 8
VMEM_LIMIT = 56 * 1024 * 1024

ROW_TILE = 512
NA_ROWS = 8
NA_TOK = NA_ROWS * GRID_W
NA_PAIRS = NA_HEADS // 2
HG_CHUNK = 32
HG_SUB = SUBLANES
HG_NSUB = HG_CHUNK // HG_SUB
HG_BLOCK = 256
HG_NCHUNK = HG_BLOCK // HG_CHUNK
FF_CHUNK = 1024
ATT_SUB = 128


def _params(*sem):
    return pltpu.CompilerParams(dimension_semantics=sem, vmem_limit_bytes=VMEM_LIMIT)


def _resident(shape):
    zeros = (0,) * len(shape)
    return pl.BlockSpec(shape, lambda *_: zeros, pipeline_mode=pl.Buffered(1))


def _proj_kernel(x_ref, w_ref, o_ref, *, n_chunk):
    xb = x_ref[...].astype(BF16)
    for c in range(0, o_ref.shape[1], n_chunk):
        acc = jnp.dot(xb, w_ref[:, c:c + n_chunk], preferred_element_type=F32)
        o_ref[:, c:c + n_chunk] = acc.astype(o_ref.dtype)


def _proj(x2d, w, tm):
    n_rows, d = x2d.shape
    n_out = w.shape[1]
    tm = min(tm, n_rows)
    assert n_rows % tm == 0
    n_chunk = 512 if n_out % 512 == 0 else n_out
    return pl.pallas_call(
        functools.partial(_proj_kernel, n_chunk=n_chunk),
        grid=(n_rows // tm,),
        in_specs=[pl.BlockSpec((tm, d), lambda i: (i, 0)), _resident((d, n_out))],
        out_specs=pl.BlockSpec((tm, n_out), lambda i: (i, 0)),
        out_shape=jax.ShapeDtypeStruct((n_rows, n_out), BF16),
        compiler_params=_params("parallel"),
        name="proj",
    )(x2d, w)


def _na_bias_table(rpb):
    qc = jnp.arange(GRID_W)[:, None]
    kc = jnp.arange(GRID_W)[None, :]
    cstart = jnp.clip(qc - WIN_C // 2, 0, GRID_W - WIN_C)
    valid = (kc >= cstart) & (kc < cstart + WIN_C)
    dcol = jnp.clip(kc - qc + WIN_C - 1, 0, 2 * WIN_C - 2)
    t = jnp.where(valid, rpb[:, :, dcol], -jnp.inf)
    nd = 2 * WIN_R - 2
    x = jnp.stack([t[:, :nd], t[:, 1:]], axis=3)
    x = x.reshape(NA_PAIRS, 2, nd, GRID_W, 2 * GRID_W)
    return jnp.transpose(x, (0, 2, 1, 3, 4)).reshape(NA_PAIRS, nd, 2 * GRID_W, 2 * GRID_W).astype(F32)


def _na_kernel(q_ref, kp_ref, kc_ref, kn_ref, vp_ref, vc_ref, vn_ref, bias_ref, o_ref, kcat, vcat, s_scr, e_scr,
               m_scr, d_scr, *, rows):
    j = pl.program_id(1)
    kcat[0:NA_TOK] = kp_ref[0]
    kcat[NA_TOK:2 * NA_TOK] = kc_ref[0]
    kcat[2 * NA_TOK:3 * NA_TOK] = kn_ref[0]
    vcat[0:NA_TOK] = vp_ref[0]
    vcat[NA_TOK:2 * NA_TOK] = vc_ref[0]
    vcat[2 * NA_TOK:3 * NA_TOK] = vn_ref[0]

    lo = lax.broadcasted_iota(jnp.int32, (1, LANES), 1) < NA_DH
    n_keys = WIN_R * GRID_W

    def window(i):
        r = j * NA_ROWS + i
        rs = jnp.clip(r - WIN_R // 2, 0, rows - WIN_R)
        start = pl.multiple_of((rs - j * NA_ROWS + NA_ROWS) * GRID_W, GRID_W)
        return r - rs, pl.ds(start, n_keys), pl.ds(pl.multiple_of(i * GRID_W, GRID_W), GRID_W)

    rep = (2 * GRID_W, LANES)

    def score_body(i, carry):
        delta, keys, qrow = window(i)
        for p in range(NA_PAIRS):
            cols = slice(p * LANES, (p + 1) * LANES)
            q = q_ref[0, qrow, cols] * (NA_DH ** -0.5)
            zero = jnp.zeros_like(q)
            q2 = jnp.concatenate([jnp.where(lo, q, zero), jnp.where(lo, zero, q)], axis=0)
            s = lax.dot_general(q2, kcat[keys, cols], (((1,), (1,)), ((), ())), preferred_element_type=F32)
            bias = jnp.concatenate(
                [bias_ref[p, WIN_R - 1 - delta + 2 * m] for m in range(WIN_R // 2)], axis=1)
            s = s + bias
            s_scr[i * NA_PAIRS + p] = s
            m_scr[i * NA_PAIRS + p] = jnp.broadcast_to(jnp.max(s, axis=-1, keepdims=True), rep)
        return carry

    lax.fori_loop(0, NA_ROWS, score_body, 0, unroll=True)

    def exp_body(u, carry):
        e = jnp.exp(s_scr[u] - jnp.tile(m_scr[u], (1, n_keys // LANES)))
        d_scr[u] = jnp.broadcast_to(jnp.sum(e, axis=-1, keepdims=True), rep)
        e_scr[u] = e.astype(BF16)
        return carry

    lax.fori_loop(0, NA_ROWS * NA_PAIRS, exp_body, 0, unroll=24)

    def value_body(i, carry):
        _, keys, qrow = window(i)
        for p in range(NA_PAIRS):
            cols = slice(p * LANES, (p + 1) * LANES)
            u = i * NA_PAIRS + p
            o = jnp.dot(e_scr[u], vcat[keys, cols], preferred_element_type=F32) / d_scr[u]
            out = jnp.where(lo, o[0:GRID_W], o[GRID_W:2 * GRID_W])
            o_ref[0, qrow, cols] = out.astype(o_ref.dtype)
        return carry

    lax.fori_loop(0, NA_ROWS, value_body, 0, unroll=True)


def _na_attn(h, bias, rows):
    bsz, t, _ = h.shape
    assert rows % NA_ROWS == 0 and rows >= WIN_R and t == rows * GRID_W
    nb = rows // NA_ROWS
    blk = (1, NA_TOK, D_PRIM)
    n_units = NA_ROWS * NA_PAIRS

    def spec(col, shift):
        return pl.BlockSpec(blk, lambda b, j: (b, jnp.clip(j + shift, 0, nb - 1), col))

    return pl.pallas_call(
        functools.partial(_na_kernel, rows=rows),
        grid=(bsz, nb),
        in_specs=[spec(0, 0), spec(1, -1), spec(1, 0), spec(1, 1), spec(2, -1), spec(2, 0), spec(2, 1),
                  _resident(bias.shape)],
        out_specs=pl.BlockSpec(blk, lambda b, j: (b, j, 0)),
        out_shape=jax.ShapeDtypeStruct((bsz, t, D_PRIM), BF16),
        scratch_shapes=[pltpu.VMEM((3 * NA_TOK, D_PRIM), BF16), pltpu.VMEM((3 * NA_TOK, D_PRIM), BF16),
                        pltpu.VMEM((n_units, 2 * GRID_W, WIN_R * GRID_W), F32),
                        pltpu.VMEM((n_units, 2 * GRID_W, WIN_R * GRID_W), BF16),
                        pltpu.VMEM((n_units, 2 * GRID_W, LANES), F32), pltpu.VMEM((n_units, 2 * GRID_W, LANES), F32)],
        compiler_params=_params("parallel", "parallel"),
        name="na_attn",
    )(h, h, h, h, h, h, h, bias)


def _hgrn_consts(rev):
    t = lax.broadcasted_iota(jnp.int32, (HG_BLOCK, HG_BLOCK), 0)
    s = lax.broadcasted_iota(jnp.int32, (HG_BLOCK, HG_BLOCK), 1)
    seen = (s >= t) if rev else (s <= t)
    tri = jnp.where(seen & (s // HG_CHUNK == t // HG_CHUNK), 1.0, 0.0).astype(BF16)
    kr = lax.broadcasted_iota(jnp.int32, (2 * LANES, LANES), 0)
    kl = lax.broadcasted_iota(jnp.int32, (2 * LANES, LANES), 1)
    fold = jnp.where(kr // LANES == kl // HG_CHUNK, 1.0, 0.0).astype(BF16)
    trow = lax.broadcasted_iota(jnp.int32, (HG_CHUNK, LANES), 0)
    lane = lax.broadcasted_iota(jnp.int32, (HG_CHUNK, LANES), 1)
    key = lane & (HG_CHUNK - 1)
    same = (key // HG_SUB == trow // HG_SUB) & (lane < 2 * HG_CHUNK)
    diag_hit = []
    for d in range(0, HG_SUB, 2):
        dist = jnp.where(lane < HG_CHUNK, d, d + 1)
        diag_hit.append(same & (key == (trow + dist if rev else trow - dist)))
    sub_lane = lax.broadcasted_iota(jnp.int32, (HG_SUB, LANES), 1)
    off_cols = [(sub_lane >= i * HG_SUB) & (sub_lane < (i + 1) * HG_SUB) for i in range(HG_NSUB)]
    return tri, fold, diag_hit, off_cols


def _hgrn_head(ql, fl, v, lb, st, consts, rev):
    tri, fold, diag_hit, off_cols = consts
    qs = ql * jax.nn.sigmoid(ql)
    f = lb + (1.0 - lb) * jax.nn.sigmoid(fl)
    k = 1.0 - f
    g = jnp.log(f)
    g_hi = g.astype(BF16)
    g_lo = (g - g_hi.astype(F32)).astype(BF16)
    b = jnp.dot(tri, jnp.concatenate([g_hi, g_lo], axis=1), preferred_element_type=F32)
    b = b[:, :LANES] + b[:, LANES:]
    yield None

    def pos(p):
        return HG_NSUB - 1 - p if rev else p

    def sub(a, c, p):
        r0 = c * HG_CHUNK + pos(p) * HG_SUB
        return a[r0:r0 + HG_SUB]

    def end(c, p):
        r = c * HG_CHUNK + pos(p) * HG_SUB + (0 if rev else HG_SUB - 1)
        return b[r:r + 1]

    pairs = [(i, j) for i in range(1, HG_NSUB) for j in range(i)]
    q_bar, k_hat, k_bar, lhs_off, decay = {}, {}, {}, [], []
    for c in range(HG_NCHUNK):
        ends = [end(c, p) for p in range(HG_NSUB)]
        q_hat = []
        for p in range(HG_NSUB):
            bb = sub(b, c, p)
            q_hat.append(sub(qs, c, p) * jnp.exp(bb if p == 0 else bb - ends[p - 1]))
            kh = sub(k, c, p) * jnp.exp(ends[p] - bb)
            k_hat[c, pos(p)] = kh
            q_bar[c, pos(p)] = q_hat[p] if p == 0 else q_hat[p] * jnp.exp(ends[p - 1])
            k_bar[c, pos(p)] = kh if p == HG_NSUB - 1 else kh * jnp.exp(ends[-1] - ends[p])
        lhs_off += [q_hat[i] if i == j + 1 else q_hat[i] * jnp.exp(ends[i - 1] - ends[j]) for i, j in pairs]
        decay.append(jnp.exp(ends[-1]))

    def rows_cat(parts):
        return jnp.concatenate([parts[c, i] for c in range(HG_NCHUNK) for i in range(HG_NSUB)], axis=0).astype(BF16)

    q_bar, k_hat, k_bar = rows_cat(q_bar), rows_cat(k_hat), rows_cat(k_bar)
    lhs_off = jnp.concatenate(lhs_off, axis=0).astype(BF16)
    vb = v.astype(BF16)

    shift = HG_SUB - 1 if rev else 1

    def prev(a):
        a3 = a.reshape(HG_BLOCK // HG_SUB, HG_SUB, LANES)
        return pltpu.roll(a3, shift, 1).reshape(HG_BLOCK, LANES)

    kd = k
    terms = [qs * k]
    for d in range(1, HG_SUB):
        kd = f * prev(kd)
        terms.append(qs * kd)
    terms = [x.astype(BF16) for x in terms]
    z = jnp.concatenate([jnp.concatenate([terms[d], terms[d + 1]], axis=1) for d in range(0, HG_SUB, 2)], axis=0)

    zpad = jnp.zeros((LANES - HG_CHUNK, LANES), BF16)
    vpad = jnp.zeros((LANES - 2 * HG_CHUNK, LANES), BF16)
    n_off = len(pairs) * HG_SUB
    chunk_rows = [slice(c * HG_CHUNK, (c + 1) * HG_CHUNK) for c in range(HG_NCHUNK)]
    r_offs = [lax.dot_general(lhs_off[c * n_off:(c + 1) * n_off], jnp.concatenate([k_hat[rs], zpad], axis=0),
                              (((1,), (1,)), ((), ())), preferred_element_type=F32)
              for c, rs in enumerate(chunk_rows)]
    rd = jnp.dot(z, fold, preferred_element_type=F32)
    upd = [lax.dot_general(vb[rs], k_bar[rs], (((0,), (0,)), ((), ())), preferred_element_type=F32)
           for rs in chunk_rows]
    yield None

    o_intra = []
    for c, rs in enumerate(chunk_rows):
        r_off = r_offs[c]
        a_rows = []
        for i in range(HG_NSUB):
            p = pos(i)
            acc = jnp.zeros((HG_SUB, LANES), F32)
            for j in range(p):
                n = pairs.index((p, j))
                acc = jnp.where(off_cols[pos(j)], r_off[n * HG_SUB:(n + 1) * HG_SUB], acc)
            a_rows.append(acc)
        a = jnp.concatenate(a_rows, axis=0)
        for n in range(HG_SUB // 2):
            a = a + jnp.where(diag_hit[n], rd[n * HG_BLOCK + c * HG_CHUNK:n * HG_BLOCK + (c + 1) * HG_CHUNK], 0.0)
        v_c = vb[rs]
        v2 = jnp.concatenate([v_c, v_c, vpad], axis=0)
        o_intra.append(jnp.dot(a.astype(BF16), v2, preferred_element_type=F32))
    yield None

    outs = [None] * HG_NCHUNK
    for step in range(HG_NCHUNK):
        c = HG_NCHUNK - 1 - step if rev else step
        outs[c] = o_intra[c] + lax.dot_general(q_bar[chunk_rows[c]], st.astype(BF16), (((1,), (1,)), ((), ())),
                                               preferred_element_type=F32)
        st = st * decay[c] + upd[c]
    yield jnp.concatenate(outs, axis=0), st


def _hgrn_kernel(lbl_ref, qf_ref, ff_ref, vf_ref, qb_ref, fb_ref, vb_ref, of_ref, ob_ref, st_f, st_b, lb_ref, tri_ref,
                 fold_ref, *, layer):
    @pl.when(pl.program_id(1) == 0)
    def _():
        st_f[...] = jnp.zeros_like(st_f)
        st_b[...] = jnp.zeros_like(st_b)
        logits = lbl_ref[...]
        mx = jnp.max(logits, axis=0, keepdims=True)
        ex = jnp.exp(logits - mx)
        prob = ex / jnp.sum(ex, axis=0, keepdims=True)
        lb_ref[...] = jnp.sum(prob[0:layer + 1], axis=0, keepdims=True) - prob[0:1]
        for n, rev in enumerate((False, True)):
            tri, fold, _, _ = _hgrn_consts(rev)
            tri_ref[n] = tri
            fold_ref[...] = fold

    consts_f = (tri_ref[0], fold_ref[...]) + _hgrn_consts(False)[2:]
    consts_b = (tri_ref[1], fold_ref[...]) + _hgrn_consts(True)[2:]

    def head(h, carry):
        cs = pl.ds(pl.multiple_of(h * LANES, LANES), LANES)
        dirs = ((qf_ref, ff_ref, vf_ref, of_ref, st_f, consts_f, False),
                (qb_ref, fb_ref, vb_ref, ob_ref, st_b, consts_b, True))
        gens = [_hgrn_head(q_ref[0, :, cs].astype(F32), f_ref[0, :, cs].astype(F32), v_ref[0, :, cs].astype(F32),
                           lb_ref[:, cs], st_ref[h], consts, rev)
                for q_ref, f_ref, v_ref, _, st_ref, consts, rev in dirs]
        results = [None, None]
        while results[0] is None:
            results = [next(g) for g in gens]
        for (_, _, _, o_ref, st_ref, _, _), (o, st) in zip(dirs, results):
            o_ref[0, :, cs] = o.astype(o_ref.dtype)
            st_ref[h] = st
        return carry

    lax.fori_loop(0, HG_HEADS, head, 0, unroll=True)


def _hgrn_scan(h, lb_logits, layer):
    bsz, t, _ = h.shape
    assert t % HG_BLOCK == 0
    nb = t // HG_BLOCK
    blk = (1, HG_BLOCK, HG_F)
    fwd = lambda col: pl.BlockSpec(blk, lambda b, j: (b, j, col))
    bwd = lambda col: pl.BlockSpec(blk, lambda b, j: (b, nb - 1 - j, col))
    out_sd = jax.ShapeDtypeStruct((bsz, t, D_PRIM), BF16)
    state = pltpu.VMEM((HG_HEADS, HG_DV, HG_DK), F32)
    return pl.pallas_call(
        functools.partial(_hgrn_kernel, layer=layer),
        grid=(bsz, nb),
        in_specs=[_resident(lb_logits.shape), fwd(0), fwd(1), fwd(3), bwd(0), bwd(2), bwd(3)],
        out_specs=[fwd(0), bwd(0)],
        out_shape=[out_sd, out_sd],
        scratch_shapes=[state, state, pltpu.VMEM((1, HG_F), F32), pltpu.VMEM((2, HG_BLOCK, HG_BLOCK), BF16),
                        pltpu.VMEM((2 * LANES, LANES), BF16)],
        compiler_params=_params("parallel", "arbitrary"),
        name="hgrn_scan",
    )(lb_logits, h, h, h, h, h, h)


def _layer_norm(y, g, b):
    mu = jnp.mean(y, axis=-1, keepdims=True)
    yc = y - mu
    var = jnp.mean(yc * yc, axis=-1, keepdims=True)
    return yc * lax.rsqrt(var + LN_EPS) * g + b


def _mem_attention(qm_ref, mkv_ref, mem_out):
    lo = lax.broadcasted_iota(jnp.int32, (1, LANES), 1) < CROSS_DH
    units = [(slice(t * ATT_SUB, (t + 1) * ATT_SUB), p)
             for t in range(qm_ref.shape[1] // ATT_SUB) for p in range(CROSS_HEADS // 2)]
    scores = []
    for rws, p in units:
        cols = slice(p * LANES, (p + 1) * LANES)
        q = qm_ref[0, rws, cols] * (CROSS_DH ** -0.5)
        zero = jnp.zeros_like(q)
        q2 = jnp.concatenate([jnp.where(lo, q, zero), jnp.where(lo, zero, q)], axis=0)
        scores.append(lax.dot_general(q2, mkv_ref[0, :, cols], (((1,), (1,)), ((), ())),
                                      preferred_element_type=F32))
    yield None
    for (rws, p), s in zip(units, scores):
        cols = slice(p * LANES, (p + 1) * LANES)
        mv = mkv_ref[0, :, D_CROSS + p * LANES:D_CROSS + (p + 1) * LANES]
        mx = jnp.max(s, axis=-1, keepdims=True)
        e = jnp.exp(s - mx)
        den = jnp.sum(e, axis=-1, keepdims=True)
        o = jnp.dot(e.astype(BF16), mv, preferred_element_type=F32) / den
        mem_out[rws, cols] = jnp.where(lo, o[0:ATT_SUB], o[ATT_SUB:2 * ATT_SUB]).astype(BF16)
    yield None


def _mix_ffn_tail(x_ref, prim, mem_attn, mem_out, wo_ref, g1_ref, b1_ref, w1_ref, w2_ref, g2_ref, b2_ref, o_ref):
    next(mem_attn)
    mix = jnp.dot(prim, wo_ref[0:D_PRIM, :], preferred_element_type=F32)
    next(mem_attn)
    mix = mix + jnp.dot(mem_out[...], wo_ref[D_PRIM:D_MODEL, :], preferred_element_type=F32)
    half = mix.shape[0] // 2
    x1s = []
    for r in range(2):
        rs = slice(r * half, (r + 1) * half)
        x1s.append(_layer_norm(ALPHA * x_ref[0, rs, :] + mix[rs], g1_ref[...], b1_ref[...]))
    for r in range(2):
        rs = slice(r * half, (r + 1) * half)
        x1 = x1s[r]
        x1b = x1.astype(BF16)
        acc = jnp.zeros(x1.shape, F32)
        for c in range(0, D_FF, FF_CHUNK):
            hid = jnp.dot(x1b, w1_ref[:, c:c + FF_CHUNK], preferred_element_type=F32)
            hid = jnp.square(jnp.maximum(hid, 0.0)).astype(BF16)
            acc = acc + jnp.dot(hid, w2_ref[c:c + FF_CHUNK, :], preferred_element_type=F32)
        o_ref[0, rs, :] = _layer_norm(ALPHA * x1 + acc, g2_ref[...], b2_ref[...])


def _mix_ffn_a_kernel(x_ref, prim_ref, qm_ref, mkv_ref, wo_ref, g1_ref, b1_ref, w1_ref, w2_ref, g2_ref, b2_ref,
                      o_ref, mem_out):
    _mix_ffn_tail(x_ref, prim_ref[0], _mem_attention(qm_ref, mkv_ref, mem_out), mem_out, wo_ref, g1_ref, b1_ref,
                  w1_ref, w2_ref, g2_ref, b2_ref, o_ref)


def _mix_ffn_b_kernel(x_ref, of_ref, ob_ref, gate_ref, nw_ref, qm_ref, mkv_ref, wo_ref, g1_ref, b1_ref, w1_ref,
                      w2_ref, g2_ref, b2_ref, o_ref, mem_out):
    o = of_ref[0].astype(F32) + ob_ref[0].astype(F32)
    gate = gate_ref[0].astype(F32)
    gate = gate * jax.nn.sigmoid(gate)
    nw = nw_ref[...]
    heads = []
    for h in range(HG_HEADS):
        cs = slice(h * HG_DV, (h + 1) * HG_DV)
        oh = o[:, cs]
        oh = oh * lax.rsqrt(jnp.mean(oh * oh, axis=-1, keepdims=True) + RMS_EPS) * nw
        heads.append((oh * gate[:, cs]).astype(BF16))
    prim = jnp.concatenate(heads, axis=1)
    _mix_ffn_tail(x_ref, prim, _mem_attention(qm_ref, mkv_ref, mem_out), mem_out, wo_ref, g1_ref, b1_ref, w1_ref,
                  w2_ref, g2_ref, b2_ref, o_ref)


def _mix_ffn(x, mixer_in, h, qm_col, mem_kv, w_out, ln1_g, ln1_b, w_ff1, w_ff2, ln2_g, ln2_b, norm_w=None):
    bsz, t, d = x.shape
    tm = ROW_TILE
    assert t % tm == 0
    tile = lambda w, col: pl.BlockSpec((1, tm, w), lambda b, i: (b, i, col))
    vec = lambda a: a.reshape(1, -1)
    common_specs = [tile(D_CROSS, qm_col), pl.BlockSpec((1, N_MEM, 2 * D_CROSS), lambda b, i: (b, 0, 0)),
                    _resident(w_out.shape), _resident((1, d)), _resident((1, d)), _resident(w_ff1.shape),
                    _resident(w_ff2.shape), _resident((1, d)), _resident((1, d))]
    common_args = [h, mem_kv, w_out, vec(ln1_g), vec(ln1_b), w_ff1, w_ff2, vec(ln2_g), vec(ln2_b)]
    if norm_w is None:
        body, name = _mix_ffn_a_kernel, "mix_ffn_a"
        specs = [tile(d, 0), tile(D_PRIM, 0)] + common_specs
        args = [x, mixer_in[0]] + common_args
    else:
        body, name = _mix_ffn_b_kernel, "mix_ffn_b"
        specs = [tile(d, 0), tile(D_PRIM, 0), tile(D_PRIM, 0), tile(D_PRIM, 4), _resident((1, HG_DV))] + common_specs
        args = [x, mixer_in[0], mixer_in[1], h, vec(norm_w)] + common_args
    return pl.pallas_call(
        body,
        grid=(bsz, t // tm),
        in_specs=specs,
        out_specs=tile(d, 0),
        out_shape=jax.ShapeDtypeStruct((bsz, t, d), F32),
        scratch_shapes=[pltpu.VMEM((tm, D_CROSS), BF16)],
        compiler_params=_params("parallel", "parallel"),
        name=name,
    )(*args)


def _trunk(x, mem, w_mem_kv, w_in_a, na_bias, w_in_b, lb_logits, hg_norm_w, w_out, ln1_g, ln1_b, w_ff1, w_ff2,
           ln2_g, ln2_b):
    bsz, t, d = x.shape
    rows = t // GRID_W
    mem_kv = _proj(mem.reshape(bsz * N_MEM, d), w_mem_kv, ROW_TILE).reshape(bsz, N_MEM, 2 * D_CROSS)
    for layer in range(DEPTH):
        j = layer // 2
        if layer % 2 == 0:
            h = _proj(x.reshape(bsz * t, d), w_in_a[j], ROW_TILE).reshape(bsz, t, -1)
            mixer_in = (_na_attn(h, na_bias[j], rows),)
            qm_col, norm_w = 3 * D_PRIM // D_CROSS, None
        else:
            h = _proj(x.reshape(bsz * t, d), w_in_b[j], ROW_TILE).reshape(bsz, t, -1)
            mixer_in = _hgrn_scan(h, lb_logits, layer)
            qm_col, norm_w = (3 * HG_F + 2 * D_PRIM) // D_CROSS, hg_norm_w[j]
        x = _mix_ffn(x, mixer_in, h, qm_col, mem_kv, w_out[layer], ln1_g[layer], ln1_b[layer], w_ff1[layer],
                     w_ff2[layer], ln2_g[layer], ln2_b[layer], norm_w)
    return x


def kernel(x_prompt, x_sample, mem_prompt, mem_sample, w_mem_kv, w_in_a, rpb, w_in_b, lb_logits, hg_norm_w, w_out,
           ln1_g, ln1_b, w_ff1, w_ff2, ln2_g, ln2_b):
    bf = lambda a: a.astype(BF16)
    na_bias = jax.vmap(_na_bias_table)(rpb)
    shared = (bf(w_mem_kv), bf(w_in_a), na_bias, bf(w_in_b), lb_logits.astype(F32), hg_norm_w, bf(w_out), ln1_g,
              ln1_b, bf(w_ff1), bf(w_ff2), ln2_g, ln2_b)
    return (_trunk(x_prompt, mem_prompt, *shared), _trunk(x_sample, mem_sample, *shared))
```
